```python
import math
import jax, jax.numpy as jnp
from jax import lax
import numpy as np

D_MODEL = 2048
BATCH = 16
SEQ = 2048
DEPTH = 2

N_META = 16
BLOCK = 128
D_FF = 5632
EPS = 1e-6
S5_GROUP = 16
S5_STATE = 64
S5_GROUPS = 32
S5_WIDTH = S5_GROUPS * S5_GROUP
MLA_HEADS = 8
MLA_Q_RANK = 512
MLA_KV_RANK = 256
MLA_NOPE = 128
MLA_ROPE = 64
MLA_QK = MLA_NOPE + MLA_ROPE
MLA_V = 128
ROPE_THETA = 10000.0
EVEN_IN = S5_WIDTH + MLA_Q_RANK + MLA_KV_RANK + MLA_ROPE
EVEN_MIX = S5_WIDTH + MLA_HEADS * MLA_V
SB_HEADS = 8
SB_DIM = 128
SB_W = SB_HEADS * SB_DIM
GDN_HEADS = 8
GDN_DK = 128
GDN_DV = 128
GDN_CONV = 4
GDN_QKV = GDN_HEADS * (2 * GDN_DK + GDN_DV)
ODD_IN = 3 * SB_W + GDN_QKV + 2 * GDN_HEADS + GDN_HEADS * GDN_DV
ODD_MIX = SB_W + GDN_HEADS * GDN_DV

kernel_name = "hybrid_s5_mla_stickbreak_gdn_macaron"


def rmsnorm(x, g):
    xf = x.astype(jnp.float32)
    y = xf * lax.rsqrt(jnp.mean(xf * xf, axis=-1, keepdims=True) + EPS)
    return (y * g.astype(jnp.float32)).astype(x.dtype)


def l2norm(x):
    xf = x.astype(jnp.float32)
    return xf * lax.rsqrt(jnp.sum(xf * xf, axis=-1, keepdims=True) + EPS)


def swiglu(x, w_gate, w_up, w_down):
    return (jax.nn.silu(x @ w_gate) * (x @ w_up)) @ w_down


def query_blocks(length):
    return [(0, N_META)] + [(s, min(s + BLOCK, length)) for s in range(N_META, length, BLOCK)]


def apply_rope(x, pos):
    half = x.shape[-1] // 2
    inv = ROPE_THETA ** (-jnp.arange(half, dtype=jnp.float32) / half)
    ang = pos.astype(jnp.float32)[:, None] * inv[None, :]
    cos = jnp.cos(ang)[None, :, None, :]
    sin = jnp.sin(ang)[None, :, None, :]
    x1 = x[..., :half].astype(jnp.float32)
    x2 = x[..., half:].astype(jnp.float32)
    return jnp.concatenate([x1 * cos - x2 * sin, x1 * sin + x2 * cos], axis=-1).astype(x.dtype)


def s5_mixer(u, log_dt, a_re, a_im, b_re, b_im, c_re, c_im, d_skip, w_glu):
    bsz, L, _ = u.shape
    f32 = jnp.float32
    dt = jnp.exp(log_dt.astype(f32))[:, None]
    ar, ai = a_re.astype(f32), a_im.astype(f32)
    mag = jnp.exp(dt * ar)
    lam_r, lam_i = mag * jnp.cos(dt * ai), mag * jnp.sin(dt * ai)
    den = ar * ar + ai * ai
    coef_r = ((lam_r - 1.0) * ar + lam_i * ai) / den
    coef_i = (lam_i * ar - (lam_r - 1.0) * ai) / den
    br, bi = b_re.astype(f32), b_im.astype(f32)
    bbar_r = coef_r[..., None] * br - coef_i[..., None] * bi
    bbar_i = coef_r[..., None] * bi + coef_i[..., None] * br
    ut = jnp.swapaxes(u.astype(f32).reshape(bsz, L, S5_GROUPS, S5_GROUP), 0, 1)
    xr = jnp.einsum('lbgc,gpc->lbgp', ut, bbar_r)
    xi = jnp.einsum('lbgc,gpc->lbgp', ut, bbar_i)
    lr_t = jnp.broadcast_to(lam_r, (L, 1, S5_GROUPS, S5_STATE))
    li_t = jnp.broadcast_to(lam_i, (L, 1, S5_GROUPS, S5_STATE))

    def combine(e1, e2):
        a1r, a1i, b1r, b1i = e1
        a2r, a2i, b2r, b2i = e2
        return (a2r * a1r - a2i * a1i, a2r * a1i + a2i * a1r,
                a2r * b1r - a2i * b1i + b2r, a2r * b1i + a2i * b1r + b2i)

    _, _, sr, si = lax.associative_scan(combine, (lr_t, li_t, xr, xi), axis=0)
    y = (jnp.einsum('lbgp,gcp->lbgc', sr, c_re.astype(f32))
         - jnp.einsum('lbgp,gcp->lbgc', si, c_im.astype(f32))
         + d_skip.astype(f32).reshape(S5_GROUPS, S5_GROUP) * ut)
    y = jnp.swapaxes(y, 0, 1).reshape(bsz, L, S5_WIDTH)
    y = jax.nn.gelu(y).astype(u.dtype)
    return y * jax.nn.sigmoid(y @ w_glu)


def mla(c_q, c_kv, k_rope, g_cq, g_ckv, w_uq, w_ukv, g_qn, g_kn):
    bsz, L, _ = c_q.shape
    pos = jnp.arange(L)
    q = (rmsnorm(c_q, g_cq) @ w_uq).reshape(bsz, L, MLA_HEADS, MLA_QK)
    kv = (rmsnorm(c_kv, g_ckv) @ w_ukv).reshape(bsz, L, MLA_HEADS, MLA_NOPE + MLA_V)
    k_nope, v = kv[..., :MLA_NOPE], kv[..., MLA_NOPE:]
    k = jnp.concatenate([k_nope, jnp.broadcast_to(k_rope[:, :, None, :], (bsz, L, MLA_HEADS, MLA_ROPE))], axis=-1)
    q = rmsnorm(q, g_qn)
    k = rmsnorm(k, g_kn)
    q = jnp.concatenate([q[..., :MLA_NOPE], apply_rope(q[..., MLA_NOPE:], pos)], axis=-1)
    k = jnp.concatenate([k[..., :MLA_NOPE], apply_rope(k[..., MLA_NOPE:], pos)], axis=-1)
    scale = MLA_QK ** -0.5
    outs = []
    for s, e in query_blocks(L):
        sc = jnp.einsum('bqhd,bkhd->bhqk', q[:, s:e], k[:, :e]).astype(jnp.float32) * scale
        causal = jnp.arange(s, e)[:, None] >= jnp.arange(e)[None, :]
        p = jax.nn.softmax(jnp.where(causal, sc, -jnp.inf), axis=-1)
        outs.append(jnp.einsum('bhqk,bkhd->bqhd', p.astype(v.dtype), v[:, :e]))
    return jnp.concatenate(outs, axis=1).reshape(bsz, L, MLA_HEADS * MLA_V)


def stick_breaking(q, k, v):
    L = q.shape[1]
    scale = SB_DIM ** -0.5
    outs = []
    for s, e in query_blocks(L):
        z = jnp.einsum('bqhd,bkhd->bhqk', q[:, s:e], k[:, :e]).astype(jnp.float32) * scale
        strict = jnp.arange(s, e)[:, None] > jnp.arange(e)[None, :]
        log_keep = jnp.where(strict, jax.nn.log_sigmoid(-z), 0.0)
        tail = lax.cumsum(log_keep, axis=3, reverse=True) - log_keep
        w = jnp.where(strict, jnp.exp(jax.nn.log_sigmoid(z) + tail), 0.0)
        outs.append(jnp.einsum('bhqk,bkhd->bqhd', w.astype(v.dtype), v[:, :e]))
    return jnp.concatenate(outs, axis=1)


def causal_depthwise_conv(x, w):
    return lax.conv_general_dilated(x, w.astype(x.dtype)[:, None, :], window_strides=(1,),
                                    padding=[(GDN_CONV - 1, 0)],
                                    dimension_numbers=('NWC', 'WIO', 'NWC'),
                                    feature_group_count=x.shape[-1])


def gdn_chunk_scan(q, k, v, g, beta, state):
    c = q.shape[3]
    G = jnp.cumsum(g, axis=-1)
    lower = jnp.tril(jnp.ones((c, c), dtype=bool))
    strict = jnp.tril(jnp.ones((c, c), dtype=bool), -1)
    decay = jnp.exp(jnp.where(lower, G[..., :, None] - G[..., None, :], -jnp.inf))
    kb = k * beta[..., None]
    m = jnp.where(strict, jnp.einsum('bhnid,bhnjd->bhnij', kb, k) * decay, 0.0)
    eye = jnp.eye(c, dtype=q.dtype)
    t_inv = lax.linalg.triangular_solve(eye + m, jnp.broadcast_to(eye, m.shape), left_side=True,
                                        lower=True, unit_diagonal=True)
    w = t_inv @ (kb * jnp.exp(G)[..., None])
    u = t_inv @ (v * beta[..., None])
    attn = jnp.einsum('bhnid,bhnjd->bhnij', q, k) * decay
    q_dec = q * jnp.exp(G)[..., None]
    k_dec = k * jnp.exp(G[..., -1:] - G)[..., None]
    g_last = jnp.exp(G[..., -1])

    def step(s, xs):
        w_c, u_c, a_c, qd_c, kd_c, gl_c = xs
        v_new = u_c - w_c @ s
        o = qd_c @ s + a_c @ v_new
        s = s * gl_c[..., None, None] + jnp.swapaxes(kd_c, -1, -2) @ v_new
        return s, o

    xs = tuple(jnp.moveaxis(t, 2, 0) for t in (w, u, attn, q_dec, k_dec, g_last))
    state, o = lax.scan(step, state, xs)
    return jnp.moveaxis(o, 0, 2), state


def gated_deltanet(qkv_in, a_in, b_in, z, conv_w, a_log, dt_bias, g_out):
    bsz, L, _ = qkv_in.shape
    H = GDN_HEADS
    f32 = jnp.float32
    qkv = jax.nn.silu(causal_depthwise_conv(qkv_in, conv_w))
    q, k, v = jnp.split(qkv, [H * GDN_DK, 2 * H * GDN_DK], axis=-1)
    q = l2norm(q.reshape(bsz, L, H, GDN_DK)) * (GDN_DK ** -0.5)
    k = l2norm(k.reshape(bsz, L, H, GDN_DK))
    v = v.reshape(bsz, L, H, GDN_DV).astype(f32)
    beta = jax.nn.sigmoid(b_in.astype(f32))
    g = -jnp.exp(a_log.astype(f32)) * jax.nn.softplus(a_in.astype(f32) + dt_bias.astype(f32))

    def split_chunks(t):
        t = jnp.swapaxes(t, 1, 2)
        meta = t[:, :, :N_META][:, :, None]
        real = t[:, :, N_META:]
        real = real.reshape(real.shape[:2] + (real.shape[2] // BLOCK, BLOCK) + real.shape[3:])
        return meta, real

    parts = [split_chunks(t) for t in (q, k, v, g, beta)]
    s0 = jnp.zeros((bsz, H, GDN_DK, GDN_DV), f32)
    o_meta, s1 = gdn_chunk_scan(*[p[0] for p in parts], s0)
    o_real, _ = gdn_chunk_scan(*[p[1] for p in parts], s1)
    o = jnp.concatenate([o_meta.reshape(bsz, H, N_META, GDN_DV),
                         o_real.reshape(bsz, H, L - N_META, GDN_DV)], axis=2)
    o = jnp.swapaxes(o, 1, 2)
    o = rmsnorm(o, g_out) * jax.nn.silu(z.astype(f32).reshape(bsz, L, H, GDN_DV))
    return o.reshape(bsz, L, H * GDN_DV).astype(qkv_in.dtype)


def even_mixer(h, w_in, log_dt, a_re, a_im, b_re, b_im, c_re, c_im, d_skip, w_glu,
               g_cq, g_ckv, w_uq, w_ukv, g_qn, g_kn, w_out):
    p = h @ w_in
    u, c_q, c_kv, k_rope = jnp.split(
        p, [S5_WIDTH, S5_WIDTH + MLA_Q_RANK, S5_WIDTH + MLA_Q_RANK + MLA_KV_RANK], axis=-1)
    y_a = s5_mixer(u, log_dt, a_re, a_im, b_re, b_im, c_re, c_im, d_skip, w_glu)
    y_b = mla(c_q, c_kv, k_rope, g_cq, g_ckv, w_uq, w_ukv, g_qn, g_kn)
    return jnp.concatenate([y_a, y_b], axis=-1) @ w_out


def odd_mixer(h, w_in, conv_w, a_log, dt_bias, g_out, w_out):
    bsz, L, _ = h.shape
    p = h @ w_in
    o1 = 3 * SB_W + GDN_QKV
    sq, sk, sv, gqkv, ga, gb, gz = jnp.split(
        p, [SB_W, 2 * SB_W, 3 * SB_W, o1, o1 + GDN_HEADS, o1 + 2 * GDN_HEADS], axis=-1)
    shp = (bsz, L, SB_HEADS, SB_DIM)
    y_c = stick_breaking(sq.reshape(shp), sk.reshape(shp), sv.reshape(shp)).reshape(bsz, L, SB_W)
    y_d = gated_deltanet(gqkv, ga, gb, gz, conv_w, a_log, dt_bias, g_out)
    return jnp.concatenate([y_c, y_d], axis=-1) @ w_out


def setup_inputs(seed: int = 0) -> dict:
    key = jax.random.key(seed)
    k = jax.random.split(key, 34)
    f32 = jnp.float32
    ne, no = (DEPTH + 1) // 2, DEPTH // 2
    G, P, C = S5_GROUPS, S5_STATE, S5_GROUP

    def nrm(i, shape, std):
        return std * jax.random.normal(k[i], shape, f32)

    def gain(i, shape):
        return 1.0 + 0.02 * jax.random.normal(k[i], shape, f32)

    dt_gdn = jnp.exp(jax.random.uniform(k[30], (no, GDN_HEADS), f32, math.log(1e-3), math.log(1e-1)))
    return {
        "x": nrm(0, (BATCH, SEQ, D_MODEL), 1.0),
        "meta_tokens": nrm(1, (N_META, D_MODEL), 1.0),
        "norm_ffn1": gain(2, (DEPTH, D_MODEL)),
        "w1_gate": nrm(3, (DEPTH, D_MODEL, D_FF), D_MODEL ** -0.5),
        "w1_up": nrm(4, (DEPTH, D_MODEL, D_FF), D_MODEL ** -0.5),
        "w1_down": nrm(5, (DEPTH, D_FF, D_MODEL), D_FF ** -0.5),
        "norm_mix": gain(10, (DEPTH, D_MODEL)),
        "norm_ffn2": gain(6, (DEPTH, D_MODEL)),
        "w2_gate": nrm(7, (DEPTH, D_MODEL, D_FF), D_MODEL ** -0.5),
        "w2_up": nrm(8, (DEPTH, D_MODEL, D_FF), D_MODEL ** -0.5),
        "w2_down": nrm(9, (DEPTH, D_FF, D_MODEL), D_FF ** -0.5),
        "ev_w_in": nrm(11, (ne, D_MODEL, EVEN_IN), D_MODEL ** -0.5),
        "s5_log_dt": jax.random.uniform(k[12], (ne, G), f32, math.log(1e-3), math.log(1e-1)),
        "s5_a_re": -0.5 + 0.01 * jax.random.uniform(k[13], (ne, G, P), f32, -1.0, 1.0),
        "s5_a_im": jnp.broadcast_to(jnp.pi * jnp.arange(P, dtype=f32), (ne, G, P)),
        "s5_b_re": nrm(14, (ne, G, P, C), (2 * C) ** -0.5),
        "s5_b_im": nrm(15, (ne, G, P, C), (2 * C) ** -0.5),
        "s5_c_re": nrm(16, (ne, G, C, P), P ** -0.5),
        "s5_c_im": nrm(17, (ne, G, C, P), P ** -0.5),
        "s5_d": nrm(18, (ne, S5_WIDTH), 1.0),
        "s5_w_glu": nrm(19, (ne, S5_WIDTH, S5_WIDTH), S5_WIDTH ** -0.5),
        "mla_g_cq": gain(20, (ne, MLA_Q_RANK)),
        "mla_g_ckv": gain(21, (ne, MLA_KV_RANK)),
        "mla_w_uq": nrm(22, (ne, MLA_Q_RANK, MLA_HEADS * MLA_QK), MLA_Q_RANK ** -0.5),
        "mla_w_ukv": nrm(23, (ne, MLA_KV_RANK, MLA_HEADS * (MLA_NOPE + MLA_V)), MLA_KV_RANK ** -0.5),
        "mla_g_q": gain(24, (ne, MLA_QK)),
        "mla_g_k": gain(25, (ne, MLA_QK)),
        "ev_w_out": nrm(26, (ne, EVEN_MIX, D_MODEL), EVEN_MIX ** -0.5),
        "od_w_in": nrm(27, (no, D_MODEL, ODD_IN), D_MODEL ** -0.5),
        "gdn_conv": nrm(28, (no, GDN_CONV, GDN_QKV), GDN_CONV ** -0.5),
        "gdn_a_log": jnp.log(jax.random.uniform(k[29], (no, GDN_HEADS), f32, 1.0, 16.0)),
        "gdn_dt_bias": dt_gdn + jnp.log(-jnp.expm1(-dt_gdn)),
        "gdn_g_out": gain(31, (no, GDN_DV)),
        "od_w_out": nrm(32, (no, ODD_MIX, D_MODEL), ODD_MIX ** -0.5),
    }


def reference(x, meta_tokens, norm_ffn1, w1_gate, w1_up, w1_down, norm_mix, norm_ffn2,
              w2_gate, w2_up, w2_down, ev_w_in, s5_log_dt, s5_a_re, s5_a_im, s5_b_re, s5_b_im,
              s5_c_re, s5_c_im, s5_d, s5_w_glu, mla_g_cq, mla_g_ckv, mla_w_uq, mla_w_ukv,
              mla_g_q, mla_g_k, ev_w_out, od_w_in, gdn_conv, gdn_a_log, gdn_dt_bias, gdn_g_out,
              od_w_out):
    bsz = x.shape[0]
    meta = jnp.broadcast_to(meta_tokens[None].astype(x.dtype), (bsz, N_META, D_MODEL))
    h = jnp.concatenate([meta, x], axis=1)
    for l in range(DEPTH):
        i = l // 2
        h = h + 0.5 * swiglu(rmsnorm(h, norm_ffn1[l]), w1_gate[l], w1_up[l], w1_down[l])
        hn = rmsnorm(h, norm_mix[l])
        if l % 2 == 0:
            mix = even_mixer(hn, ev_w_in[i], s5_log_dt[i], s5_a_re[i], s5_a_im[i], s5_b_re[i],
                             s5_b_im[i], s5_c_re[i], s5_c_im[i], s5_d[i], s5_w_glu[i],
                             mla_g_cq[i], mla_g_ckv[i], mla_w_uq[i], mla_w_ukv[i],
                             mla_g_q[i], mla_g_k[i], ev_w_out[i])
        else:
            mix = odd_mixer(hn, od_w_in[i], gdn_conv[i], gdn_a_log[i], gdn_dt_bias[i],
                            gdn_g_out[i], od_w_out[i])
        h = h + mix
        h = h + 0.5 * swiglu(rmsnorm(h, norm_ffn2[l]), w2_gate[l], w2_up[l], w2_down[l])
    return h[:, N_META:]
```

```python
import functools
import math

import jax
import jax.numpy as jnp
from jax import lax
from jax.experimental import pallas as pl
from jax.experimental.pallas import tpu as pltpu

F32 = jnp.float32
BF16 = jnp.bfloat16
EPS = 1e-6
N_META = 16
LANES = 128
VMEM_LIMIT_BYTES = 56 * 2**20
NEG_BIG = -1e30

S5_GROUP, S5_STATE, S5_GROUPS = 16, 64, 32
S5_WIDTH = S5_GROUPS * S5_GROUP
S5_BUNDLE_GROUPS = LANES // S5_GROUP
S5_BUNDLES = S5_GROUPS // S5_BUNDLE_GROUPS
S5_BSTATE = S5_BUNDLE_GROUPS * S5_STATE
S5_CHUNK = 128
MLA_HEADS, MLA_Q_RANK, MLA_KV_RANK = 8, 512, 256
MLA_NOPE, MLA_ROPE, MLA_V = 128, 64, 128
MLA_QK = MLA_NOPE + MLA_ROPE
ROPE_THETA = 10000.0
ATT_BLOCK = 256
SB_HEADS, SB_DIM = 8, 128
SB_W = SB_HEADS * SB_DIM
GDN_HEADS, GDN_DK, GDN_DV, GDN_CONV = 8, 128, 128, 4
GDN_CHUNK = 128
GDN_QKV = GDN_HEADS * (2 * GDN_DK + GDN_DV)


def _cparams(*sem):
    return pltpu.CompilerParams(dimension_semantics=sem, vmem_limit_bytes=VMEM_LIMIT_BYTES)


def _row_tile(rows, target):
    best = None
    for t in range(16, min(rows, target) + 1, 16):
        if rows % t == 0:
            best = t
    assert best is not None, rows
    return best


def _rmsnorm(x, g):
    return x * lax.rsqrt(jnp.mean(x * x, axis=-1, keepdims=True) + EPS) * g


def _sigmoid(x):
    return 1.0 / (1.0 + jnp.exp(-x))


def _dot(a, b):
    return jnp.dot(a, b, preferred_element_type=F32)


def _dot_nt(a, b):
    return lax.dot_general(a, b, (((1,), (1,)), ((), ())), preferred_element_type=F32)


def _ffn_kernel(x_ref, g_ref, wg_ref, wu_ref, wd_ref, o_ref, xn_ref):
    j = pl.program_id(1)

    @pl.when(j == 0)
    def _():
        xn_ref[...] = _rmsnorm(x_ref[...], g_ref[...]).astype(BF16)
        o_ref[...] = jnp.zeros_like(o_ref)

    xn = xn_ref[...]
    gate = _dot(xn, wg_ref[...])
    up = _dot(xn, wu_ref[...])
    act = (gate * _sigmoid(gate) * up).astype(BF16)
    o_ref[...] += _dot(act, wd_ref[...])

    @pl.when(j == pl.num_programs(1) - 1)
    def _():
        o_ref[...] = x_ref[...] + 0.5 * o_ref[...]


def _ffn(h, gain, wg, wu, wd, *, tm_target=768, tf=512):
    rows, d = h.shape
    f = wg.shape[1]
    tm = _row_tile(rows, tm_target)
    assert f % tf == 0
    return pl.pallas_call(
        _ffn_kernel,
        grid=(rows // tm, f // tf),
        in_specs=[
            pl.BlockSpec((tm, d), lambda i, j: (i, 0)),
            pl.BlockSpec((1, d), lambda i, j: (0, 0)),
            pl.BlockSpec((d, tf), lambda i, j: (0, j)),
            pl.BlockSpec((d, tf), lambda i, j: (0, j)),
            pl.BlockSpec((tf, d), lambda i, j: (j, 0)),
        ],
        out_specs=pl.BlockSpec((tm, d), lambda i, j: (i, 0)),
        out_shape=jax.ShapeDtypeStruct((rows, d), F32),
        scratch_shapes=[pltpu.VMEM((tm, d), BF16)],
        compiler_params=_cparams("parallel", "arbitrary"),
        name="ffn",
    )(h, gain.reshape(1, d), wg, wu, wd)


def _norm_matmul_kernel(x_ref, g_ref, w_ref, o_ref):
    xn = _rmsnorm(x_ref[...], g_ref[...]).astype(BF16)
    o_ref[...] = _dot(xn, w_ref[...]).astype(o_ref.dtype)


def _norm_matmul(h, gain, w, *, tn, tm_target=768, name="norm_matmul"):
    rows, d = h.shape
    n = w.shape[1]
    tm = _row_tile(rows, tm_target)
    assert n % tn == 0
    return pl.pallas_call(
        _norm_matmul_kernel,
        grid=(n // tn, rows // tm),
        in_specs=[
            pl.BlockSpec((tm, d), lambda j, i: (i, 0)),
            pl.BlockSpec((1, d), lambda j, i: (0, 0)),
            pl.BlockSpec((d, tn), lambda j, i: (0, j)),
        ],
        out_specs=pl.BlockSpec((tm, tn), lambda j, i: (i, j)),
        out_shape=jax.ShapeDtypeStruct((rows, n), F32),
        compiler_params=_cparams("parallel", "parallel"),
        name=name,
    )(h, gain.reshape(1, d), w)


def _out_proj_kernel(h_ref, ya_ref, yb_ref, wa_ref, wb_ref, *rest, glu):
    if glu:
        wglu_ref, o_ref = rest
        y = ya_ref[...]
        gate = _dot(y, wglu_ref[...])
        ya = (y.astype(F32) * _sigmoid(gate)).astype(BF16)
    else:
        (o_ref,) = rest
        ya = ya_ref[...]
    o_ref[...] = h_ref[...] + _dot(ya, wa_ref[...]) + _dot(yb_ref[...], wb_ref[...])


def _out_proj(h, ya, yb, wa, wb, wglu=None, *, tm_target=768, name="out_proj"):
    rows, d = h.shape
    ka, kb = ya.shape[1], yb.shape[1]
    tm = _row_tile(rows, tm_target)
    glu = wglu is not None
    in_specs = [
        pl.BlockSpec((tm, d), lambda i: (i, 0)),
        pl.BlockSpec((tm, ka), lambda i: (i, 0)),
        pl.BlockSpec((tm, kb), lambda i: (i, 0)),
        pl.BlockSpec((ka, d), lambda i: (0, 0)),
        pl.BlockSpec((kb, d), lambda i: (0, 0)),
    ]
    args = [h, ya, yb, wa, wb]
    if glu:
        in_specs.append(pl.BlockSpec((ka, ka), lambda i: (0, 0)))
        args.append(wglu)
    return pl.pallas_call(
        functools.partial(_out_proj_kernel, glu=glu),
        grid=(rows // tm,),
        in_specs=in_specs,
        out_specs=pl.BlockSpec((tm, d), lambda i: (i, 0)),
        out_shape=jax.ShapeDtypeStruct((rows, d), F32),
        compiler_params=_cparams("parallel"),
        name=name,
    )(*args)


def _cmul(ar, ai, br, bi):
    return ar * br - ai * bi, ar * bi + ai * br


def _s5_kernel(u_ref, bb_ref, cm_ref, tneg_ref, tpos_ref, tpos1_ref, d_ref, o_ref, x_s, s_s, *, seq):
    n = S5_BSTATE
    c = S5_CHUNK
    u = u_ref[0]
    x_s[...] = _dot(u.astype(BF16), bb_ref[0])

    def tri(size):
        r = lax.broadcasted_iota(jnp.int32, (size, size), 0)
        col = lax.broadcasted_iota(jnp.int32, (size, size), 1)
        return (r >= col).astype(BF16)

    def chunk(r0, size, ltri, carry):
        cr, ci = carry
        x = x_s[pl.ds(r0, size), :]
        tn = tneg_ref[0, pl.ds(0, size), :]
        tp = tpos_ref[0, pl.ds(0, size), :]
        tq = tpos1_ref[0, pl.ds(0, size), :]
        xr, xi = _cmul(x[:, :n], x[:, n:], tn[:, :n], tn[:, n:])
        acc = _dot(ltri, jnp.concatenate([xr, xi], axis=1).astype(BF16))
        sr, si = _cmul(acc[:, :n], acc[:, n:], tp[:, :n], tp[:, n:])
        qr, qi = _cmul(tq[:, :n], tq[:, n:], cr, ci)
        sr = sr + qr
        si = si + qi
        s_s[pl.ds(r0, size), :] = jnp.concatenate([sr, si], axis=1).astype(BF16)
        return sr[size - 1:size, :], si[size - 1:size, :]

    nfull = seq // c
    tail = seq - nfull * c
    ltri = tri(c)
    zero = jnp.zeros((1, n), F32)
    carry = lax.fori_loop(
        0, nfull, lambda i, cy: chunk(pl.multiple_of(i * c, c), c, ltri, cy), (zero, zero))
    if tail:
        chunk(nfull * c, tail, tri(tail), carry)
    y = _dot(s_s[...], cm_ref[0]) + d_ref[0] * u
    o_ref[0] = jax.nn.gelu(y, approximate=True).astype(o_ref.dtype)


def _s5(p3d, tabs):
    bsz, seq, _ = p3d.shape
    n2 = 2 * S5_BSTATE
    tab_spec = pl.BlockSpec((1, S5_CHUNK, n2), lambda b, k: (k, 0, 0))
    return pl.pallas_call(
        functools.partial(_s5_kernel, seq=seq),
        grid=(bsz, S5_BUNDLES),
        in_specs=[
            pl.BlockSpec((1, seq, LANES), lambda b, k: (b, 0, k)),
            pl.BlockSpec((1, LANES, n2), lambda b, k: (k, 0, 0)),
            pl.BlockSpec((1, n2, LANES), lambda b, k: (k, 0, 0)),
            tab_spec, tab_spec, tab_spec,
            pl.BlockSpec((1, 1, LANES), lambda b, k: (k, 0, 0)),
        ],
        out_specs=pl.BlockSpec((1, seq, LANES), lambda b, k: (b, 0, k)),
        out_shape=jax.ShapeDtypeStruct((bsz, seq, S5_WIDTH), BF16),
        scratch_shapes=[pltpu.VMEM((seq, n2), F32), pltpu.VMEM((seq, n2), BF16)],
        compiler_params=_cparams("parallel", "parallel"),
        name="s5",
    )(p3d, tabs["bbar"], tabs["cmat"], tabs["tneg"], tabs["tpos"], tabs["tpos1"], tabs["dskip"])


def _s5_tables(log_dt, a_re, a_im, b_re, b_im, c_re, c_im, d_skip):
    g, p, c = S5_GROUPS, S5_STATE, S5_GROUP
    nb, gb = S5_BUNDLES, S5_BUNDLE_GROUPS
    dt = jnp.exp(log_dt.astype(F32))[:, None]
    ar, ai = a_re.astype(F32), a_im.astype(F32)
    mag = jnp.exp(dt * ar)
    lam_r, lam_i = mag * jnp.cos(dt * ai), mag * jnp.sin(dt * ai)
    den = ar * ar + ai * ai
    coef_r = ((lam_r - 1.0) * ar + lam_i * ai) / den
    coef_i = (lam_i * ar - (lam_r - 1.0) * ai) / den
    br, bi = b_re.astype(F32), b_im.astype(F32)
    bbar_r = coef_r[..., None] * br - coef_i[..., None] * bi
    bbar_i = coef_r[..., None] * bi + coef_i[..., None] * br
    eye = jnp.eye(gb, dtype=F32)

    def block_in(m):
        m = m.reshape(nb, gb, p, c)
        return jnp.einsum("kgpc,gh->kgchp", m, eye).reshape(nb, gb * c, gb * p)

    def block_out(m):
        m = m.reshape(nb, gb, c, p)
        return jnp.einsum("kgcp,gh->kgphc", m, eye).reshape(nb, gb * p, gb * c)

    bbar = jnp.concatenate([block_in(bbar_r), block_in(bbar_i)], axis=2).astype(BF16)
    cmat = jnp.concatenate([block_out(c_re.astype(F32)), -block_out(c_im.astype(F32))], axis=1).astype(BF16)

    def power_table(offset, sign):
        e = sign * (jnp.arange(S5_CHUNK, dtype=F32) + offset)[:, None, None]
        m = jnp.exp(e * (dt * ar)[None])
        re = (m * jnp.cos(e * (dt * ai)[None])).reshape(S5_CHUNK, nb, gb * p)
        im = (m * jnp.sin(e * (dt * ai)[None])).reshape(S5_CHUNK, nb, gb * p)
        return jnp.transpose(jnp.concatenate([re, im], axis=2), (1, 0, 2))

    return {
        "bbar": bbar, "cmat": cmat,
        "tneg": power_table(0.0, -1.0), "tpos": power_table(0.0, 1.0), "tpos1": power_table(1.0, 1.0),
        "dskip": d_skip.astype(F32).reshape(nb, 1, gb * c),
    }


def _softmax_step(s, v, m, l, acc):
    m_new = jnp.maximum(m, jnp.max(s, axis=-1, keepdims=True))
    p = jnp.exp(s - m_new)
    alpha = jnp.exp(m - m_new)
    l = alpha * l + jnp.sum(p, axis=-1, keepdims=True)
    acc = alpha * acc + _dot(p.astype(BF16), v)
    return m_new, l, acc


def _mla_kernel(cq_ref, ckv_ref, kr_ref, gcq_ref, gckv_ref, wq_ref, wkv_ref, gq_ref, gkn_ref, gkr_ref,
                cs_ref, o_ref, q_s, k_s, v_s, *, seq):
    blk = ATT_BLOCK
    lane = lax.broadcasted_iota(jnp.int32, (seq, LANES), 1)
    first_half = lane < MLA_ROPE
    cs = cs_ref[...]

    def rope_part(t):
        t = t * cs
        t = t + pltpu.roll(t, MLA_ROPE, axis=1)
        return jnp.where(first_half, t, 0.0)

    def head_norm(nope, rope_ext):
        ss = (jnp.sum(nope * nope, axis=-1, keepdims=True)
              + jnp.sum(jnp.where(first_half, rope_ext * rope_ext, 0.0), axis=-1, keepdims=True))
        return lax.rsqrt(ss * (1.0 / MLA_QK) + EPS)

    cqn = _rmsnorm(cq_ref[0], gcq_ref[...]).astype(BF16)
    xq = _dot(cqn, wq_ref[0])
    sq = head_norm(xq[:, :LANES], xq[:, LANES:]) * (MLA_QK ** -0.5)
    gq = gq_ref[...]
    q_s[...] = jnp.concatenate(
        [xq[:, :LANES] * sq * gq[:, :LANES], rope_part(xq[:, LANES:] * sq * gq[:, LANES:])],
        axis=1).astype(BF16)

    ckvn = _rmsnorm(ckv_ref[0], gckv_ref[...]).astype(BF16)
    xkv = _dot(ckvn, wkv_ref[0])
    kr = kr_ref[0]
    sk = head_norm(xkv[:, :LANES], kr)
    k_s[...] = jnp.concatenate(
        [xkv[:, :LANES] * sk * gkn_ref[...], rope_part(kr * sk * gkr_ref[...])], axis=1).astype(BF16)
    v_s[...] = xkv[:, LANES:].astype(BF16)

    def q_block(r0, rows, nprev):
        q = q_s[pl.ds(r0, rows), :]

        def body(j, carry):
            c0 = pl.multiple_of(j * blk, blk)
            s = _dot_nt(q, k_s[pl.ds(c0, blk), :])
            return _softmax_step(s, v_s[pl.ds(c0, blk), :], *carry)

        init = (jnp.full((rows, 1), NEG_BIG, F32), jnp.zeros((rows, 1), F32), jnp.zeros((rows, MLA_V), F32))
        carry = lax.fori_loop(0, nprev, body, init)
        s = _dot_nt(q, k_s[pl.ds(r0, rows), :])
        r = lax.broadcasted_iota(jnp.int32, (rows, rows), 0)
        c = lax.broadcasted_iota(jnp.int32, (rows, rows), 1)
        s = jnp.where(c <= r, s, NEG_BIG)
        _, l, acc = _softmax_step(s, v_s[pl.ds(r0, rows), :], *carry)
        o_ref[0, pl.ds(r0, rows), :] = (acc / l).astype(o_ref.dtype)

    nfull = seq // blk
    tail = seq - nfull * blk

    def outer(i, _):
        q_block(pl.multiple_of(i * blk, blk), blk, i)
        return 0

    lax.fori_loop(0, nfull, outer, 0)
    if tail:
        q_block(nfull * blk, tail, nfull)


def _mla(p3d, g_cq, g_ckv, wq, wkv, gq, gkn, gkr, cs):
    bsz, seq, _ = p3d.shape
    full = lambda shape: pl.BlockSpec(shape, lambda b, h: (0,) * len(shape))
    return pl.pallas_call(
        functools.partial(_mla_kernel, seq=seq),
        grid=(bsz, MLA_HEADS),
        in_specs=[
            pl.BlockSpec((1, seq, MLA_Q_RANK), lambda b, h: (b, 0, S5_WIDTH // MLA_Q_RANK)),
            pl.BlockSpec((1, seq, MLA_KV_RANK), lambda b, h: (b, 0, (S5_WIDTH + MLA_Q_RANK) // MLA_KV_RANK)),
            pl.BlockSpec((1, seq, LANES), lambda b, h: (b, 0, (S5_WIDTH + MLA_Q_RANK + MLA_KV_RANK) // LANES)),
            full((1, MLA_Q_RANK)), full((1, MLA_KV_RANK)),
            pl.BlockSpec((1, MLA_Q_RANK, 2 * LANES), lambda b, h: (h, 0, 0)),
            pl.BlockSpec((1, MLA_KV_RANK, 2 * LANES), lambda b, h: (h, 0, 0)),
            full((1, 2 * LANES)), full((1, LANES)), full((1, LANES)),
            full((seq, LANES)),
        ],
        out_specs=pl.BlockSpec((1, seq, MLA_V), lambda b, h: (b, 0, h)),
        out_shape=jax.ShapeDtypeStruct((bsz, seq, MLA_HEADS * MLA_V), BF16),
        scratch_shapes=[pltpu.VMEM((seq, 2 * LANES), BF16), pltpu.VMEM((seq, 2 * LANES), BF16),
                        pltpu.VMEM((seq, MLA_V), BF16)],
        compiler_params=_cparams("parallel", "arbitrary"),
        name="mla",
    )(p3d, p3d, p3d, g_cq, g_ckv, wq, wkv, gq, gkn, gkr, cs)


def _rot_cols(w):
    half = MLA_ROPE // 2
    return jnp.concatenate([-w[..., half:], w[..., :half]], axis=-1)


def _swap_halves(g):
    half = MLA_ROPE // 2
    return jnp.concatenate([g[..., half:], g[..., :half]], axis=-1)


def _mla_params(w_uq, w_ukv, g_q, g_k, seq):
    wq = w_uq.reshape(MLA_Q_RANK, MLA_HEADS, MLA_QK)
    wq = jnp.concatenate([wq, _rot_cols(wq[..., MLA_NOPE:])], axis=-1)
    wq = jnp.transpose(wq, (1, 0, 2)).astype(BF16)
    wkv = jnp.transpose(w_ukv.reshape(MLA_KV_RANK, MLA_HEADS, MLA_NOPE + MLA_V), (1, 0, 2)).astype(BF16)
    gq = jnp.concatenate([g_q, _swap_halves(g_q[MLA_NOPE:])]).reshape(1, 2 * LANES).astype(F32)
    gkn = g_k[:MLA_NOPE].reshape(1, LANES).astype(F32)
    gkr = jnp.concatenate([g_k[MLA_NOPE:], _swap_halves(g_k[MLA_NOPE:])]).reshape(1, LANES).astype(F32)
    half = MLA_ROPE // 2
    inv = ROPE_THETA ** (-jnp.arange(half, dtype=F32) / half)
    ang = jnp.arange(seq, dtype=F32)[:, None] * inv[None, :]
    cs = jnp.concatenate([jnp.cos(ang), jnp.cos(ang), jnp.sin(ang), jnp.sin(ang)], axis=1)
    return wq, wkv, gq, gkn, gkr, cs


def _split2(x):
    hi = x.astype(BF16)
    return hi, (x - hi.astype(F32)).astype(BF16)


def _split3(x):
    hi = x.astype(BF16)
    r = x - hi.astype(F32)
    mid = r.astype(BF16)
    return hi, mid, (r - mid.astype(F32)).astype(BF16)


def _sb_kernel(q_ref, k_ref, v_ref, o_ref, q_s, k_s, v_s, *, seq):
    blk = ATT_BLOCK
    q_s[...] = (q_ref[0] * (SB_DIM ** -0.5)).astype(BF16)
    k_s[...] = k_ref[0].astype(BF16)
    v_s[...] = v_ref[0].astype(BF16)

    def later_keys(size):
        r = lax.broadcasted_iota(jnp.int32, (size, size), 0)
        c = lax.broadcasted_iota(jnp.int32, (size, size), 1)
        return (r > c).astype(BF16)

    def tile(q, c0, size, u_mat, strict, tail, acc):
        z = _dot_nt(q, k_s[pl.ds(c0, size), :])
        log_keep = -(jnp.maximum(z, 0.0) + jnp.log(1.0 + jnp.exp(-jnp.abs(z))))
        if strict is not None:
            log_keep = jnp.where(strict, log_keep, 0.0)
        hi, lo = _split2(log_keep)
        inner = _dot(hi, u_mat) + _dot(lo, u_mat)
        w = jnp.exp(z + log_keep + inner + tail)
        if strict is not None:
            w = jnp.where(strict, w, 0.0)
        acc = acc + _dot(w.astype(BF16), v_s[pl.ds(c0, size), :])
        return tail + jnp.sum(log_keep, axis=-1, keepdims=True), acc

    u_blk = later_keys(blk)

    def q_block(r0, rows, nprev, u_diag):
        q = q_s[pl.ds(r0, rows), :]
        r = lax.broadcasted_iota(jnp.int32, (rows, rows), 0)
        c = lax.broadcasted_iota(jnp.int32, (rows, rows), 1)
        carry = tile(q, r0, rows, u_diag, c < r, jnp.zeros((rows, 1), F32), jnp.zeros((rows, SB_DIM), F32))

        def body(jj, cy):
            c0 = pl.multiple_of((nprev - 1 - jj) * blk, blk)
            return tile(q, c0, blk, u_blk, None, *cy)

        _, acc = lax.fori_loop(0, nprev, body, carry)
        o_ref[0, pl.ds(r0, rows), :] = acc.astype(o_ref.dtype)

    nfull = seq // blk
    tail_rows = seq - nfull * blk

    def outer(i, _):
        q_block(pl.multiple_of(i * blk, blk), blk, i, u_blk)
        return 0

    lax.fori_loop(0, nfull, outer, 0)
    if tail_rows:
        q_block(nfull * blk, tail_rows, nfull, later_keys(tail_rows))


def _stick_breaking(p3d):
    bsz, seq, _ = p3d.shape
    spec = lambda off: pl.BlockSpec((1, seq, SB_DIM), lambda b, h: (b, 0, off + h))
    return pl.pallas_call(
        functools.partial(_sb_kernel, seq=seq),
        grid=(bsz, SB_HEADS),
        in_specs=[spec(0), spec(SB_HEADS), spec(2 * SB_HEADS)],
        out_specs=pl.BlockSpec((1, seq, SB_DIM), lambda b, h: (b, 0, h)),
        out_shape=jax.ShapeDtypeStruct((bsz, seq, SB_W), BF16),
        scratch_shapes=[pltpu.VMEM((seq, SB_DIM), BF16)] * 3,
        compiler_params=_cparams("parallel", "parallel"),
        name="stick_breaking",
    )(p3d, p3d, p3d)


def _gdn_kernel(q_ref, k_ref, v_ref, z_ref, ab_ref, cwq_ref, cwk_ref, cwv_ref, gate_ref, gout_ref, o_ref,
                pad_s, q_s, k_s, v_s, g_s, b_s, o_s, *, seq):
    c = GDN_CHUNK
    h = pl.program_id(1)
    nchunks = pl.cdiv(seq, c)
    padded = nchunks * c

    def conv_silu(x_ref, w_ref):
        pad_s[pl.ds(0, 8), :] = jnp.zeros((8, LANES), F32)
        pad_s[pl.ds(8, seq), :] = x_ref[0]
        w = w_ref[...]
        y = w[GDN_CONV - 1:GDN_CONV, :] * pad_s[pl.ds(8, seq), :]
        for j in range(GDN_CONV - 1):
            y = y + w[j:j + 1, :] * pad_s[pl.ds(8 - (GDN_CONV - 1) + j, seq), :]
        return y * _sigmoid(y)

    def l2norm(x):
        return x * lax.rsqrt(jnp.sum(x * x, axis=-1, keepdims=True) + EPS)

    zeros_tail = jnp.zeros((padded - seq, LANES), F32)
    for ref in (q_s, k_s, v_s, g_s, b_s):
        if padded > seq:
            ref[pl.ds(seq, padded - seq), :] = zeros_tail
    q_s[pl.ds(0, seq), :] = l2norm(conv_silu(q_ref, cwq_ref)) * (GDN_DK ** -0.5)
    k_s[pl.ds(0, seq), :] = l2norm(conv_silu(k_ref, cwk_ref))
    v_s[pl.ds(0, seq), :] = conv_silu(v_ref, cwv_ref)

    ab = ab_ref[0]
    lane = lax.broadcasted_iota(jnp.int32, (seq, LANES), 1)
    gate = gate_ref[...]
    t = ab + gate[1:2, :]
    g_all = -jnp.exp(gate[0:1, :]) * (jnp.maximum(t, 0.0) + jnp.log(1.0 + jnp.exp(-jnp.abs(t))))
    g_col = jnp.sum(jnp.where(lane == h, g_all, 0.0), axis=-1, keepdims=True)
    b_col = jnp.sum(jnp.where(lane == h + GDN_HEADS, _sigmoid(ab), 0.0), axis=-1, keepdims=True)
    g_s[pl.ds(0, seq), :] = jnp.broadcast_to(g_col, (seq, LANES))
    b_s[pl.ds(0, seq), :] = jnp.broadcast_to(b_col, (seq, LANES))

    r = lax.broadcasted_iota(jnp.int32, (c, c), 0)
    col = lax.broadcasted_iota(jnp.int32, (c, c), 1)
    lower = r >= col
    strict = r > col
    incl = lower.astype(BF16)
    eye = (r == col).astype(F32)

    def chunk(i, state):
        r0 = pl.multiple_of(i * c, c)
        q = q_s[pl.ds(r0, c), :]
        k = k_s[pl.ds(r0, c), :]
        v = v_s[pl.ds(r0, c), :]
        gb = g_s[pl.ds(r0, c), :]
        beta = b_s[pl.ds(r0, c), :]
        parts = _split3(gb) + _split3(jnp.where(strict, gb, 0.0))
        sums = _dot(incl, jnp.concatenate(parts, axis=1))
        cum = sums[:, 0:c] + sums[:, c:2 * c] + sums[:, 2 * c:3 * c]
        dm = sums[:, 3 * c:4 * c] + sums[:, 4 * c:5 * c] + sums[:, 5 * c:6 * c]
        decay = jnp.exp(jnp.where(lower, dm, NEG_BIG))
        kb = k * beta
        kbf = k.astype(BF16)
        m = jnp.where(strict, _dot_nt(kb.astype(BF16), kbf) * decay, 0.0)
        inv = eye - m
        power = m
        for _ in range(int(math.log2(c)) - 1):
            pb = power.astype(BF16)
            power = _dot(pb, pb)
            inv = inv + _dot(inv.astype(BF16), power.astype(BF16))
        inv = inv.astype(BF16)
        e_cum = jnp.exp(cum)
        w = _dot(inv, (kb * e_cum).astype(BF16))
        u = _dot(inv, (v * beta).astype(BF16))
        attn = jnp.where(lower, _dot_nt(q.astype(BF16), kbf) * decay, 0.0)
        last = cum[c - 1:c, :]
        k_dec = k * jnp.exp(last - cum)
        sb = state.astype(BF16)
        v_new = u - _dot(w.astype(BF16), sb)
        o_s[pl.ds(r0, c), :] = _dot((q * e_cum).astype(BF16), sb) + _dot(attn.astype(BF16), v_new.astype(BF16))
        return state * jnp.exp(last) + _dot(k_dec.T.astype(BF16), v_new.astype(BF16))

    lax.fori_loop(0, nchunks, chunk, jnp.zeros((GDN_DK, GDN_DV), F32))
    o = o_s[pl.ds(0, seq), :]
    z = z_ref[0]
    o_ref[0] = (_rmsnorm(o, gout_ref[...]) * (z * _sigmoid(z))).astype(o_ref.dtype)


def _gated_deltanet(p3d, conv_w, gate, g_out):
    bsz, seq, _ = p3d.shape
    hh = GDN_HEADS
    off = 3 * SB_HEADS
    spec = lambda o: pl.BlockSpec((1, seq, LANES), lambda b, h: (b, 0, o + h))
    cspec = lambda o: pl.BlockSpec((GDN_CONV, LANES), lambda b, h: (0, o + h))
    padded = pl.cdiv(seq, GDN_CHUNK) * GDN_CHUNK
    return pl.pallas_call(
        functools.partial(_gdn_kernel, seq=seq),
        grid=(bsz, hh),
        in_specs=[
            spec(off), spec(off + hh), spec(off + 2 * hh), spec(off + 3 * hh),
            pl.BlockSpec((1, seq, LANES), lambda b, h: (b, 0, off + 4 * hh)),
            cspec(0), cspec(hh), cspec(2 * hh),
            pl.BlockSpec((2, LANES), lambda b, h: (0, 0)),
            pl.BlockSpec((1, LANES), lambda b, h: (0, 0)),
        ],
        out_specs=pl.BlockSpec((1, seq, LANES), lambda b, h: (b, 0, h)),
        out_shape=jax.ShapeDtypeStruct((bsz, seq, hh * GDN_DV), BF16),
        scratch_shapes=[pltpu.VMEM((seq + 8, LANES), F32)] + [pltpu.VMEM((padded, LANES), F32)] * 6,
        compiler_params=_cparams("parallel", "parallel"),
        name="gated_deltanet",
    )(p3d, p3d, p3d, p3d, p3d, conv_w, conv_w, conv_w, gate, g_out)


def _even_mixer(h, bsz, seq, gain, w_in, s5_params, w_glu, g_cq, g_ckv, w_uq, w_ukv, g_q, g_k, w_out):
    d = h.shape[1]
    rope0 = S5_WIDTH + MLA_Q_RANK + MLA_KV_RANK
    w_ext = jnp.concatenate([w_in, _rot_cols(w_in[:, rope0:])], axis=1).astype(BF16)
    p = _norm_matmul(h, gain, w_ext, tn=w_ext.shape[1], name="even_in_proj")
    p3d = p.reshape(bsz, seq, -1)
    y_a = _s5(p3d, _s5_tables(*s5_params))
    wq, wkv, gq, gkn, gkr, cs = _mla_params(w_uq, w_ukv, g_q, g_k, seq)
    y_b = _mla(p3d, g_cq.reshape(1, -1), g_ckv.reshape(1, -1), wq, wkv, gq, gkn, gkr, cs)
    return _out_proj(h, y_a.reshape(-1, S5_WIDTH), y_b.reshape(-1, MLA_HEADS * MLA_V),
                     w_out[:S5_WIDTH].astype(BF16), w_out[S5_WIDTH:].astype(BF16), w_glu.astype(BF16),
                     name="even_out_proj")


def _odd_mixer(h, bsz, seq, gain, w_in, conv_w, a_log, dt_bias, g_out, w_out):
    d = h.shape[1]
    o1 = 3 * SB_W + GDN_QKV
    ab = w_in[:, o1:o1 + 2 * GDN_HEADS]
    w_ext = jnp.concatenate(
        [w_in[:, :o1], w_in[:, o1 + 2 * GDN_HEADS:], ab, jnp.zeros((d, LANES - 2 * GDN_HEADS), w_in.dtype)],
        axis=1).astype(BF16)
    p = _norm_matmul(h, gain, w_ext, tn=w_ext.shape[1] // 3, tm_target=384, name="odd_in_proj")
    p3d = p.reshape(bsz, seq, -1)
    y_c = _stick_breaking(p3d)
    pad = jnp.zeros((LANES - GDN_HEADS,), F32)
    gate = jnp.stack([jnp.concatenate([a_log.astype(F32), pad]), jnp.concatenate([dt_bias.astype(F32), pad])])
    y_d = _gated_deltanet(p3d, conv_w.astype(F32), gate, g_out.reshape(1, -1).astype(F32))
    return _out_proj(h, y_c.reshape(-1, SB_W), y_d.reshape(-1, GDN_HEADS * GDN_DV),
                     w_out[:SB_W].astype(BF16), w_out[SB_W:].astype(BF16), name="odd_out_proj")


def kernel(x, meta_tokens, norm_ffn1, w1_gate, w1_up, w1_down, norm_mix, norm_ffn2, w2_gate, w2_up, w2_down, ev_w_in, s5_log_dt, s5_a_re, s5_a_im, s5_b_re, s5_b_im, s5_c_re, s5_c_im, s5_d, s5_w_glu, mla_g_cq, mla_g_ckv, mla_w_uq, mla_w_ukv, mla_g_q, mla_g_k, ev_w_out, od_w_in, gdn_conv, gdn_a_log, gdn_dt_bias, gdn_g_out, od_w_out):
    bsz, _, d = x.shape
    depth = norm_ffn1.shape[0]
    meta = jnp.broadcast_to(meta_tokens[None].astype(x.dtype), (bsz, N_META, d))
    h3 = jnp.concatenate([meta, x], axis=1)
    seq = h3.shape[1]
    h = h3.reshape(bsz * seq, d)
    for l in range(depth):
        i = l // 2
        h = _ffn(h, norm_ffn1[l], w1_gate[l].astype(BF16), w1_up[l].astype(BF16), w1_down[l].astype(BF16))
        if l % 2 == 0:
            s5_params = (s5_log_dt[i], s5_a_re[i], s5_a_im[i], s5_b_re[i], s5_b_im[i], s5_c_re[i],
                         s5_c_im[i], s5_d[i])
            h = _even_mixer(h, bsz, seq, norm_mix[l], ev_w_in[i], s5_params, s5_w_glu[i], mla_g_cq[i],
                            mla_g_ckv[i], mla_w_uq[i], mla_w_ukv[i], mla_g_q[i], mla_g_k[i], ev_w_out[i])
        else:
            h = _odd_mixer(h, bsz, seq, norm_mix[l], od_w_in[i], gdn_conv[i], gdn_a_log[i], gdn_dt_bias[i],
                           gdn_g_out[i], od_w_out[i])
        h = _ffn(h, norm_ffn2[l], w2_gate[l].astype(BF16), w2_up[l].astype(BF16), w2_down[l].astype(BF16))
    return h.reshape(bsz, seq, d)[:, N_META:]
```

```python
import functools
import math

import jax
import jax.numpy as jnp
from jax import lax
from jax.experimental import pallas as pl
from jax.experimental.pallas import tpu as pltpu

F32 = jnp.float32
BF16 = jnp.bfloat16
EPS = 1e-6
N_META = 16
LANES = 128
VMEM_LIMIT_BYTES = 56 * 2**20
NEG_BIG = -1e30

S5_GROUP, S5_STATE, S5_GROUPS = 16, 64, 32
S5_WIDTH = S5_GROUPS * S5_GROUP
S5_BUNDLE_GROUPS = LANES // S5_GROUP
S5_BUNDLES = S5_GROUPS // S5_BUNDLE_GROUPS
S5_BSTATE = S5_BUNDLE_GROUPS * S5_STATE
S5_CHUNK = 128
MLA_HEADS, MLA_Q_RANK, MLA_KV_RANK = 8, 512, 256
MLA_NOPE, MLA_ROPE, MLA_V = 128, 64, 128
MLA_QK = MLA_NOPE + MLA_ROPE
ROPE_THETA = 10000.0
ATT_BLOCK = 256
ATT_Q_BLOCK = 512
SB_HEADS, SB_DIM = 8, 128
SB_W = SB_HEADS * SB_DIM
GDN_HEADS, GDN_DK, GDN_DV, GDN_CONV = 8, 128, 128, 4
GDN_CHUNK = 128
GDN_UNROLL = 8
GDN_QKV = GDN_HEADS * (2 * GDN_DK + GDN_DV)


def _cparams(*sem):
    return pltpu.CompilerParams(dimension_semantics=sem, vmem_limit_bytes=VMEM_LIMIT_BYTES)


def _row_tile(rows, target):
    best = None
    for t in range(16, min(rows, target) + 1, 16):
        if rows % t == 0:
            best = t
    assert best is not None, rows
    return best


def _rmsnorm(x, g):
    return x * lax.rsqrt(jnp.mean(x * x, axis=-1, keepdims=True) + EPS) * g


def _sigmoid(x):
    return 1.0 / (1.0 + jnp.exp(-x))


def _dot(a, b):
    return jnp.dot(a, b, preferred_element_type=F32)


def _aligned(x, m):
    return x if isinstance(x, int) else pl.multiple_of(x, m)


def _dot_nt(a, b):
    return lax.dot_general(a, b, (((1,), (1,)), ((), ())), preferred_element_type=F32)


def _ffn_kernel(x_ref, g_ref, wg_ref, wu_ref, wd_ref, o_ref, xn_ref):
    j = pl.program_id(1)

    @pl.when(j == 0)
    def _():
        xn_ref[...] = _rmsnorm(x_ref[...], g_ref[...]).astype(BF16)
        o_ref[...] = jnp.zeros_like(o_ref)

    xn = xn_ref[...]
    gate = _dot(xn, wg_ref[...])
    up = _dot(xn, wu_ref[...])
    act = (gate * _sigmoid(gate) * up).astype(BF16)
    o_ref[...] += _dot(act, wd_ref[...])

    @pl.when(j == pl.num_programs(1) - 1)
    def _():
        o_ref[...] = x_ref[...] + 0.5 * o_ref[...]


def _ffn(h, gain, wg, wu, wd, *, tm_target=768, tf=512):
    rows, d = h.shape
    f = wg.shape[1]
    tm = _row_tile(rows, tm_target)
    assert f % tf == 0
    return pl.pallas_call(
        _ffn_kernel,
        grid=(rows // tm, f // tf),
        in_specs=[
            pl.BlockSpec((tm, d), lambda i, j: (i, 0)),
            pl.BlockSpec((1, d), lambda i, j: (0, 0)),
            pl.BlockSpec((d, tf), lambda i, j: (0, j)),
            pl.BlockSpec((d, tf), lambda i, j: (0, j)),
            pl.BlockSpec((tf, d), lambda i, j: (j, 0)),
        ],
        out_specs=pl.BlockSpec((tm, d), lambda i, j: (i, 0)),
        out_shape=jax.ShapeDtypeStruct((rows, d), F32),
        scratch_shapes=[pltpu.VMEM((tm, d), BF16)],
        compiler_params=_cparams("parallel", "arbitrary"),
        name="ffn",
    )(h, gain.reshape(1, d), wg, wu, wd)


def _norm_matmul_kernel(x_ref, g_ref, w_ref, o_ref):
    xn = _rmsnorm(x_ref[...], g_ref[...]).astype(BF16)
    o_ref[...] = _dot(xn, w_ref[...]).astype(o_ref.dtype)


def _norm_matmul(h, gain, w, *, tn, tm_target=768, name="norm_matmul"):
    rows, d = h.shape
    n = w.shape[1]
    tm = _row_tile(rows, tm_target)
    assert n % tn == 0
    return pl.pallas_call(
        _norm_matmul_kernel,
        grid=(n // tn, rows // tm),
        in_specs=[
            pl.BlockSpec((tm, d), lambda j, i: (i, 0)),
            pl.BlockSpec((1, d), lambda j, i: (0, 0)),
            pl.BlockSpec((d, tn), lambda j, i: (0, j)),
        ],
        out_specs=pl.BlockSpec((tm, tn), lambda j, i: (i, j)),
        out_shape=jax.ShapeDtypeStruct((rows, n), F32),
        compiler_params=_cparams("parallel", "parallel"),
        name=name,
    )(h, gain.reshape(1, d), w)


def _out_proj_kernel(h_ref, ya_ref, yb_ref, wa_ref, wb_ref, *rest, glu):
    if glu:
        wglu_ref, o_ref = rest
        y = ya_ref[...]
        gate = _dot(y, wglu_ref[...])
        ya = (y.astype(F32) * _sigmoid(gate)).astype(BF16)
    else:
        (o_ref,) = rest
        ya = ya_ref[...]
    o_ref[...] = h_ref[...] + _dot(ya, wa_ref[...]) + _dot(yb_ref[...], wb_ref[...])


def _out_proj(h, ya, yb, wa, wb, wglu=None, *, tm_target=768, name="out_proj"):
    rows, d = h.shape
    ka, kb = ya.shape[1], yb.shape[1]
    tm = _row_tile(rows, tm_target)
    glu = wglu is not None
    in_specs = [
        pl.BlockSpec((tm, d), lambda i: (i, 0)),
        pl.BlockSpec((tm, ka), lambda i: (i, 0)),
        pl.BlockSpec((tm, kb), lambda i: (i, 0)),
        pl.BlockSpec((ka, d), lambda i: (0, 0)),
        pl.BlockSpec((kb, d), lambda i: (0, 0)),
    ]
    args = [h, ya, yb, wa, wb]
    if glu:
        in_specs.append(pl.BlockSpec((ka, ka), lambda i: (0, 0)))
        args.append(wglu)
    return pl.pallas_call(
        functools.partial(_out_proj_kernel, glu=glu),
        grid=(rows // tm,),
        in_specs=in_specs,
        out_specs=pl.BlockSpec((tm, d), lambda i: (i, 0)),
        out_shape=jax.ShapeDtypeStruct((rows, d), F32),
        compiler_params=_cparams("parallel"),
        name=name,
    )(*args)


def _cmul(ar, ai, br, bi):
    return ar * br - ai * bi, ar * bi + ai * br


def _s5_kernel(u_ref, bb_ref, cm_ref, tneg_ref, tpos_ref, tpos1_ref, d_ref, o_ref, x_s, s_s, *, seq):
    n = S5_BSTATE
    c = S5_CHUNK
    u = u_ref[0]
    x_s[...] = _dot(u.astype(BF16), bb_ref[0])

    def tri(size):
        r = lax.broadcasted_iota(jnp.int32, (size, size), 0)
        col = lax.broadcasted_iota(jnp.int32, (size, size), 1)
        return (r >= col).astype(BF16)

    def chunk(r0, size, ltri, carry):
        cr, ci = carry
        x = x_s[pl.ds(r0, size), :]
        tn = tneg_ref[0, pl.ds(0, size), :]
        tp = tpos_ref[0, pl.ds(0, size), :]
        tq = tpos1_ref[0, pl.ds(0, size), :]
        xr, xi = _cmul(x[:, :n], x[:, n:], tn[:, :n], tn[:, n:])
        acc = _dot(ltri, jnp.concatenate([xr, xi], axis=1).astype(BF16))
        sr, si = _cmul(acc[:, :n], acc[:, n:], tp[:, :n], tp[:, n:])
        qr, qi = _cmul(tq[:, :n], tq[:, n:], cr, ci)
        sr = sr + qr
        si = si + qi
        s_s[pl.ds(r0, size), :] = jnp.concatenate([sr, si], axis=1).astype(BF16)
        return sr[size - 1:size, :], si[size - 1:size, :]

    nfull = seq // c
    tail = seq - nfull * c
    ltri = tri(c)
    zero = jnp.zeros((1, n), F32)
    carry = lax.fori_loop(
        0, nfull, lambda i, cy: chunk(pl.multiple_of(i * c, c), c, ltri, cy), (zero, zero))
    if tail:
        chunk(nfull * c, tail, tri(tail), carry)
    y = _dot(s_s[...], cm_ref[0]) + d_ref[0] * u
    o_ref[0] = jax.nn.gelu(y, approximate=True).astype(o_ref.dtype)


def _s5(p3d, tabs):
    bsz, seq, _ = p3d.shape
    n2 = 2 * S5_BSTATE
    tab_spec = pl.BlockSpec((1, S5_CHUNK, n2), lambda b, k: (k, 0, 0))
    return pl.pallas_call(
        functools.partial(_s5_kernel, seq=seq),
        grid=(bsz, S5_BUNDLES),
        in_specs=[
            pl.BlockSpec((1, seq, LANES), lambda b, k: (b, 0, k)),
            pl.BlockSpec((1, LANES, n2), lambda b, k: (k, 0, 0)),
            pl.BlockSpec((1, n2, LANES), lambda b, k: (k, 0, 0)),
            tab_spec, tab_spec, tab_spec,
            pl.BlockSpec((1, 1, LANES), lambda b, k: (k, 0, 0)),
        ],
        out_specs=pl.BlockSpec((1, seq, LANES), lambda b, k: (b, 0, k)),
        out_shape=jax.ShapeDtypeStruct((bsz, seq, S5_WIDTH), BF16),
        scratch_shapes=[pltpu.VMEM((seq, n2), F32), pltpu.VMEM((seq, n2), BF16)],
        compiler_params=_cparams("parallel", "parallel"),
        name="s5",
    )(p3d, tabs["bbar"], tabs["cmat"], tabs["tneg"], tabs["tpos"], tabs["tpos1"], tabs["dskip"])


def _s5_tables(log_dt, a_re, a_im, b_re, b_im, c_re, c_im, d_skip):
    g, p, c = S5_GROUPS, S5_STATE, S5_GROUP
    nb, gb = S5_BUNDLES, S5_BUNDLE_GROUPS
    dt = jnp.exp(log_dt.astype(F32))[:, None]
    ar, ai = a_re.astype(F32), a_im.astype(F32)
    mag = jnp.exp(dt * ar)
    lam_r, lam_i = mag * jnp.cos(dt * ai), mag * jnp.sin(dt * ai)
    den = ar * ar + ai * ai
    coef_r = ((lam_r - 1.0) * ar + lam_i * ai) / den
    coef_i = (lam_i * ar - (lam_r - 1.0) * ai) / den
    br, bi = b_re.astype(F32), b_im.astype(F32)
    bbar_r = coef_r[..., None] * br - coef_i[..., None] * bi
    bbar_i = coef_r[..., None] * bi + coef_i[..., None] * br
    eye = jnp.eye(gb, dtype=F32)

    def block_in(m):
        m = m.reshape(nb, gb, p, c)
        return jnp.einsum("kgpc,gh->kgchp", m, eye).reshape(nb, gb * c, gb * p)

    def block_out(m):
        m = m.reshape(nb, gb, c, p)
        return jnp.einsum("kgcp,gh->kgphc", m, eye).reshape(nb, gb * p, gb * c)

    bbar = jnp.concatenate([block_in(bbar_r), block_in(bbar_i)], axis=2).astype(BF16)
    cmat = jnp.concatenate([block_out(c_re.astype(F32)), -block_out(c_im.astype(F32))], axis=1).astype(BF16)

    def power_table(offset, sign):
        e = sign * (jnp.arange(S5_CHUNK, dtype=F32) + offset)[:, None, None]
        m = jnp.exp(e * (dt * ar)[None])
        re = (m * jnp.cos(e * (dt * ai)[None])).reshape(S5_CHUNK, nb, gb * p)
        im = (m * jnp.sin(e * (dt * ai)[None])).reshape(S5_CHUNK, nb, gb * p)
        return jnp.transpose(jnp.concatenate([re, im], axis=2), (1, 0, 2))

    return {
        "bbar": bbar, "cmat": cmat,
        "tneg": power_table(0.0, -1.0), "tpos": power_table(0.0, 1.0), "tpos1": power_table(1.0, 1.0),
        "dskip": d_skip.astype(F32).reshape(nb, 1, gb * c),
    }


def _query_blocks(seq, qb, kb):
    assert qb % kb == 0 and seq >= qb and seq % qb <= kb
    nloop = seq // qb - 1
    last_r0 = nloop * qb
    last_rows = seq - last_r0
    diag = [(d * kb, kb) for d in range(qb // kb)]
    last_diag = diag + ([(qb, last_rows - qb)] if last_rows > qb else [])
    return nloop, last_r0, last_rows, diag, last_diag


def _softmax_step(s, v, m, l, acc):
    m_new = jnp.maximum(m, jnp.max(s, axis=-1, keepdims=True))
    p = jnp.exp(s - m_new)
    alpha = jnp.exp(m - m_new)
    l = alpha * l + jnp.sum(p, axis=-1, keepdims=True)
    acc = alpha * acc + _dot(p.astype(BF16), v)
    return m_new, l, acc


def _mla_kernel(cq_ref, ckv_ref, kr_ref, gcq_ref, gckv_ref, wq_ref, wkv_ref, gq_ref, gkn_ref, gkr_ref,
                cs_ref, o_ref, cqn_s, ckvn_s, krope_s, krss_s, q_s, k_s, v_s, *, seq):
    qb, kb = ATT_Q_BLOCK, ATT_BLOCK
    lane = lax.broadcasted_iota(jnp.int32, (seq, LANES), 1)
    first_half = lane < MLA_ROPE
    cs = cs_ref[...]

    def rope_part(t):
        t = t * cs
        t = t + pltpu.roll(t, MLA_ROPE, axis=1)
        return jnp.where(first_half, t, 0.0)

    @pl.when(pl.program_id(1) == 0)
    def _():
        cqn_s[...] = _rmsnorm(cq_ref[0], gcq_ref[...]).astype(BF16)
        ckvn_s[...] = _rmsnorm(ckv_ref[0], gckv_ref[...]).astype(BF16)
        kr = kr_ref[0]
        krope_s[...] = rope_part(kr * gkr_ref[...])
        ss = jnp.sum(jnp.where(first_half, kr * kr, 0.0), axis=-1, keepdims=True)
        krss_s[...] = jnp.broadcast_to(ss, (seq, LANES))

    xq = _dot(cqn_s[...], wq_ref[0])
    xn, xr = xq[:, :LANES], xq[:, LANES:]
    ss = jnp.sum(xn * xn + jnp.where(first_half, xr * xr, 0.0), axis=-1, keepdims=True)
    sq = lax.rsqrt(ss * (1.0 / MLA_QK) + EPS) * (MLA_QK ** -0.5)
    gq = gq_ref[...]
    q_s[...] = jnp.concatenate(
        [xn * (sq * gq[:, :LANES]), rope_part(xr * (sq * gq[:, LANES:]))], axis=1).astype(BF16)

    xkv = _dot(ckvn_s[...], wkv_ref[0])
    kn = xkv[:, :LANES]
    ss = jnp.sum(kn * kn, axis=-1, keepdims=True) + krss_s[...]
    sk = lax.rsqrt(ss * (1.0 / MLA_QK) + EPS)
    k_s[...] = jnp.concatenate([kn * (sk * gkn_ref[...]), krope_s[...] * sk], axis=1).astype(BF16)
    v_s[...] = xkv[:, LANES:].astype(BF16)

    def q_block(r0, rows, nprev, diag):
        q = q_s[pl.ds(r0, rows), :]

        def body(j, carry):
            c0 = pl.multiple_of(j * kb, kb)
            s = _dot_nt(q, k_s[pl.ds(c0, kb), :])
            return _softmax_step(s, v_s[pl.ds(c0, kb), :], *carry)

        init = (jnp.full((rows, 1), NEG_BIG, F32), jnp.zeros((rows, 1), F32), jnp.zeros((rows, MLA_V), F32))
        carry = lax.fori_loop(0, nprev, body, init)
        for off, size in diag:
            s = _dot_nt(q, k_s[pl.ds(r0 + off, size), :])
            r = lax.broadcasted_iota(jnp.int32, (rows, size), 0)
            c = lax.broadcasted_iota(jnp.int32, (rows, size), 1)
            s = jnp.where(c + off <= r, s, NEG_BIG)
            carry = _softmax_step(s, v_s[pl.ds(r0 + off, size), :], *carry)
        _, l, acc = carry
        o_ref[0, pl.ds(r0, rows), :] = (acc / l).astype(o_ref.dtype)

    nloop, last_r0, last_rows, diag, last_diag = _query_blocks(seq, qb, kb)

    def outer(i, _):
        q_block(pl.multiple_of(i * qb, qb), qb, i * (qb // kb), diag)
        return 0

    lax.fori_loop(0, nloop, outer, 0)
    q_block(last_r0, last_rows, last_r0 // kb, last_diag)


def _mla(p3d, g_cq, g_ckv, wq, wkv, gq, gkn, gkr, cs):
    bsz, seq, _ = p3d.shape
    full = lambda shape: pl.BlockSpec(shape, lambda b, h: (0,) * len(shape))
    return pl.pallas_call(
        functools.partial(_mla_kernel, seq=seq),
        grid=(bsz, MLA_HEADS),
        in_specs=[
            pl.BlockSpec((1, seq, MLA_Q_RANK), lambda b, h: (b, 0, S5_WIDTH // MLA_Q_RANK)),
            pl.BlockSpec((1, seq, MLA_KV_RANK), lambda b, h: (b, 0, (S5_WIDTH + MLA_Q_RANK) // MLA_KV_RANK)),
            pl.BlockSpec((1, seq, LANES), lambda b, h: (b, 0, (S5_WIDTH + MLA_Q_RANK + MLA_KV_RANK) // LANES)),
            full((1, MLA_Q_RANK)), full((1, MLA_KV_RANK)),
            pl.BlockSpec((1, MLA_Q_RANK, 2 * LANES), lambda b, h: (h, 0, 0)),
            pl.BlockSpec((1, MLA_KV_RANK, 2 * LANES), lambda b, h: (h, 0, 0)),
            full((1, 2 * LANES)), full((1, LANES)), full((1, LANES)),
            full((seq, LANES)),
        ],
        out_specs=pl.BlockSpec((1, seq, MLA_V), lambda b, h: (b, 0, h)),
        out_shape=jax.ShapeDtypeStruct((bsz, seq, MLA_HEADS * MLA_V), BF16),
        scratch_shapes=[pltpu.VMEM((seq, MLA_Q_RANK), BF16), pltpu.VMEM((seq, MLA_KV_RANK), BF16),
                        pltpu.VMEM((seq, LANES), F32), pltpu.VMEM((seq, LANES), F32),
                        pltpu.VMEM((seq, 2 * LANES), BF16), pltpu.VMEM((seq, 2 * LANES), BF16),
                        pltpu.VMEM((seq, MLA_V), BF16)],
        compiler_params=_cparams("parallel", "arbitrary"),
        name="mla",
    )(p3d, p3d, p3d, g_cq, g_ckv, wq, wkv, gq, gkn, gkr, cs)


def _rot_cols(w):
    half = MLA_ROPE // 2
    return jnp.concatenate([-w[..., half:], w[..., :half]], axis=-1)


def _swap_halves(g):
    half = MLA_ROPE // 2
    return jnp.concatenate([g[..., half:], g[..., :half]], axis=-1)


def _mla_params(w_uq, w_ukv, g_q, g_k, seq):
    wq = w_uq.reshape(MLA_Q_RANK, MLA_HEADS, MLA_QK)
    wq = jnp.concatenate([wq, _rot_cols(wq[..., MLA_NOPE:])], axis=-1)
    wq = jnp.transpose(wq, (1, 0, 2)).astype(BF16)
    wkv = jnp.transpose(w_ukv.reshape(MLA_KV_RANK, MLA_HEADS, MLA_NOPE + MLA_V), (1, 0, 2)).astype(BF16)
    gq = jnp.concatenate([g_q, _swap_halves(g_q[MLA_NOPE:])]).reshape(1, 2 * LANES).astype(F32)
    gkn = g_k[:MLA_NOPE].reshape(1, LANES).astype(F32)
    gkr = jnp.concatenate([g_k[MLA_NOPE:], _swap_halves(g_k[MLA_NOPE:])]).reshape(1, LANES).astype(F32)
    half = MLA_ROPE // 2
    inv = ROPE_THETA ** (-jnp.arange(half, dtype=F32) / half)
    ang = jnp.arange(seq, dtype=F32)[:, None] * inv[None, :]
    cs = jnp.concatenate([jnp.cos(ang), jnp.cos(ang), jnp.sin(ang), jnp.sin(ang)], axis=1)
    return wq, wkv, gq, gkn, gkr, cs


def _split2(x):
    hi = x.astype(BF16)
    return hi, (x - hi.astype(F32)).astype(BF16)


def _split3(x):
    hi = x.astype(BF16)
    r = x - hi.astype(F32)
    mid = r.astype(BF16)
    return hi, mid, (r - mid.astype(F32)).astype(BF16)


def _sb_kernel(q_ref, k_ref, v_ref, o_ref, q_s, k_s, v_s, *, seq):
    qb, blk = ATT_Q_BLOCK, ATT_BLOCK
    q_s[...] = (q_ref[0] * (SB_DIM ** -0.5)).astype(BF16)
    k_s[...] = k_ref[0].astype(BF16)
    v_s[...] = v_ref[0].astype(BF16)

    def later_keys(size):
        r = lax.broadcasted_iota(jnp.int32, (size, size), 0)
        c = lax.broadcasted_iota(jnp.int32, (size, size), 1)
        return (r > c).astype(BF16)

    def tile(q, c0, size, u_mat, strict, tail, acc):
        z = _dot_nt(q, k_s[pl.ds(c0, size), :])
        log_keep = -(jnp.maximum(z, 0.0) + jnp.log(1.0 + jnp.exp(-jnp.abs(z))))
        if strict is not None:
            log_keep = jnp.where(strict, log_keep, 0.0)
        hi, lo = _split2(log_keep)
        inner = _dot(hi, u_mat) + _dot(lo, u_mat)
        w = jnp.exp(z + log_keep + inner + tail)
        if strict is not None:
            w = jnp.where(strict, w, 0.0)
        acc = acc + _dot(w.astype(BF16), v_s[pl.ds(c0, size), :])
        return tail + jnp.sum(log_keep, axis=-1, keepdims=True), acc

    u_blk = later_keys(blk)

    def q_block(r0, rows, nprev, diag):
        q = q_s[pl.ds(r0, rows), :]
        carry = (jnp.zeros((rows, 1), F32), jnp.zeros((rows, SB_DIM), F32))
        for off, size, u_mat in reversed(diag):
            r = lax.broadcasted_iota(jnp.int32, (rows, size), 0)
            c = lax.broadcasted_iota(jnp.int32, (rows, size), 1)
            carry = tile(q, r0 + off, size, u_mat, c + off < r, *carry)

        def body(jj, cy):
            c0 = pl.multiple_of((nprev - 1 - jj) * blk, blk)
            return tile(q, c0, blk, u_blk, None, *cy)

        _, acc = lax.fori_loop(0, nprev, body, carry)
        o_ref[0, pl.ds(r0, rows), :] = acc.astype(o_ref.dtype)

    nloop, last_r0, last_rows, diag, last_diag = _query_blocks(seq, qb, blk)
    u_of = lambda size: u_blk if size == blk else later_keys(size)

    def outer(i, _):
        q_block(pl.multiple_of(i * qb, qb), qb, i * (qb // blk), [(o, s, u_of(s)) for o, s in diag])
        return 0

    lax.fori_loop(0, nloop, outer, 0)
    q_block(last_r0, last_rows, last_r0 // blk, [(o, s, u_of(s)) for o, s in last_diag])


def _stick_breaking(p3d):
    bsz, seq, _ = p3d.shape
    spec = lambda off: pl.BlockSpec((1, seq, SB_DIM), lambda b, h: (b, 0, off + h))
    return pl.pallas_call(
        functools.partial(_sb_kernel, seq=seq),
        grid=(bsz, SB_HEADS),
        in_specs=[spec(0), spec(SB_HEADS), spec(2 * SB_HEADS)],
        out_specs=pl.BlockSpec((1, seq, SB_DIM), lambda b, h: (b, 0, h)),
        out_shape=jax.ShapeDtypeStruct((bsz, seq, SB_W), BF16),
        scratch_shapes=[pltpu.VMEM((seq, SB_DIM), BF16)] * 3,
        compiler_params=_cparams("parallel", "parallel"),
        name="stick_breaking",
    )(p3d, p3d, p3d)


def _gdn_kernel(q_ref, k_ref, v_ref, z_ref, ab_ref, cwq_ref, cwk_ref, cwv_ref, gate_ref, gout_ref, o_ref,
                pad_s, q_s, k_s, v_s, g_s, b_s, o_s, u_s, w_s, a_s, qd_s, kdt_s, gl_s, *, seq):
    c = GDN_CHUNK
    h = pl.program_id(1)
    nchunks = pl.cdiv(seq, c)
    padded = nchunks * c

    def conv_silu(x_ref, w_ref):
        pad_s[pl.ds(0, 8), :] = jnp.zeros((8, LANES), F32)
        pad_s[pl.ds(8, seq), :] = x_ref[0]
        w = w_ref[...]
        y = w[GDN_CONV - 1:GDN_CONV, :] * pad_s[pl.ds(8, seq), :]
        for j in range(GDN_CONV - 1):
            y = y + w[j:j + 1, :] * pad_s[pl.ds(8 - (GDN_CONV - 1) + j, seq), :]
        return y * _sigmoid(y)

    def l2norm(x):
        return x * lax.rsqrt(jnp.sum(x * x, axis=-1, keepdims=True) + EPS)

    zeros_tail = jnp.zeros((padded - seq, LANES), F32)
    for ref in (q_s, k_s, v_s, g_s, b_s):
        if padded > seq:
            ref[pl.ds(seq, padded - seq), :] = zeros_tail
    q_s[pl.ds(0, seq), :] = l2norm(conv_silu(q_ref, cwq_ref)) * (GDN_DK ** -0.5)
    k_s[pl.ds(0, seq), :] = l2norm(conv_silu(k_ref, cwk_ref))
    v_s[pl.ds(0, seq), :] = conv_silu(v_ref, cwv_ref)

    ab = ab_ref[0]
    lane = lax.broadcasted_iota(jnp.int32, (seq, LANES), 1)
    gate = gate_ref[...]
    t = ab + gate[1:2, :]
    g_all = -jnp.exp(gate[0:1, :]) * (jnp.maximum(t, 0.0) + jnp.log(1.0 + jnp.exp(-jnp.abs(t))))
    g_col = jnp.sum(jnp.where(lane == h, g_all, 0.0), axis=-1, keepdims=True)
    b_col = jnp.sum(jnp.where(lane == h + GDN_HEADS, _sigmoid(ab), 0.0), axis=-1, keepdims=True)
    g_s[pl.ds(0, seq), :] = jnp.broadcast_to(g_col, (seq, LANES))
    b_s[pl.ds(0, seq), :] = jnp.broadcast_to(b_col, (seq, LANES))

    r = lax.broadcasted_iota(jnp.int32, (c, c), 0)
    col = lax.broadcasted_iota(jnp.int32, (c, c), 1)
    lower = r >= col
    strict = r > col
    incl = lower.astype(BF16)
    eye = (r == col).astype(F32)
    level_masks = [((r >> l) == (col >> l)) & ((r >> (l - 1)) != (col >> (l - 1))) & strict
                   for l in range(1, int(math.log2(c)) + 1)]

    def local(first, count):
        ids = [first + t for t in range(count)]
        rows = [pl.ds(_aligned(i * c, c), c) for i in ids]
        ks = [k_s[r, :] for r in rows]
        betas = [b_s[r, :] for r in rows]
        gbs = [g_s[r, :] for r in rows]
        sums = [_dot(incl, jnp.concatenate(_split3(gb) + _split3(jnp.where(strict, gb, 0.0)), axis=1))
                for gb in gbs]
        cums = [s[:, 0:c] + s[:, c:2 * c] + s[:, 2 * c:3 * c] for s in sums]
        decays = [jnp.exp(jnp.where(lower, s[:, 3 * c:4 * c] + s[:, 4 * c:5 * c] + s[:, 5 * c:6 * c], NEG_BIG))
                  for s in sums]
        kbs = [k * beta for k, beta in zip(ks, betas)]
        kbfs = [k.astype(BF16) for k in ks]
        ms = [jnp.where(strict, _dot_nt(kb.astype(BF16), kbf) * decay, 0.0)
              for kb, kbf, decay in zip(kbs, kbfs, decays)]
        invs = [eye.astype(BF16)] * count
        for mask in level_masks:
            ts = [_dot(jnp.where(mask, m, 0.0).astype(BF16), inv).astype(BF16) for m, inv in zip(ms, invs)]
            invs = [(inv.astype(F32) - _dot(inv, t)).astype(BF16) for inv, t in zip(invs, ts)]
        e_cums = [jnp.exp(cum) for cum in cums]
        for r, inv, kb, e_cum in zip(rows, invs, kbs, e_cums):
            w_s[r, :] = _dot(inv, (kb * e_cum).astype(BF16)).astype(BF16)
        for r, inv, beta in zip(rows, invs, betas):
            u_s[r, :] = _dot(inv, (v_s[r, :] * beta).astype(BF16))
        qs = [q_s[r, :] for r in rows]
        for r, q, kbf, decay in zip(rows, qs, kbfs, decays):
            a_s[r, :] = jnp.where(lower, _dot_nt(q.astype(BF16), kbf) * decay, 0.0).astype(BF16)
        for i, r, q, k, cum, e_cum in zip(ids, rows, qs, ks, cums, e_cums):
            qd_s[r, :] = (q * e_cum).astype(BF16)
            last = cum[c - 1:c, :]
            kdt_s[r, :] = (k * jnp.exp(last - cum)).T.astype(BF16)
            gl_s[pl.ds(_aligned(i * 8, 8), 8), :] = jnp.broadcast_to(jnp.exp(last), (8, LANES))

    def recur(i, state):
        r0 = pl.multiple_of(i * c, c)
        sb = state.astype(BF16)
        v_new = (u_s[pl.ds(r0, c), :] - _dot(w_s[pl.ds(r0, c), :], sb)).astype(BF16)
        o_s[pl.ds(r0, c), :] = _dot(qd_s[pl.ds(r0, c), :], sb) + _dot(a_s[pl.ds(r0, c), :], v_new)
        g_last = gl_s[pl.ds(pl.multiple_of(i * 8, 8), 1), :]
        return state * g_last + _dot(kdt_s[pl.ds(r0, c), :], v_new)

    ngroups = max(nchunks // GDN_UNROLL, 1)
    for gi in range(ngroups):
        first = gi * GDN_UNROLL
        local(first, GDN_UNROLL if gi < ngroups - 1 else nchunks - first)
    lax.fori_loop(0, nchunks, recur, jnp.zeros((GDN_DK, GDN_DV), F32))
    o = o_s[pl.ds(0, seq), :]
    z = z_ref[0]
    o_ref[0] = (_rmsnorm(o, gout_ref[...]) * (z * _sigmoid(z))).astype(o_ref.dtype)


def _gated_deltanet(p3d, conv_w, gate, g_out):
    bsz, seq, _ = p3d.shape
    hh = GDN_HEADS
    off = 3 * SB_HEADS
    spec = lambda o: pl.BlockSpec((1, seq, LANES), lambda b, h: (b, 0, o + h))
    cspec = lambda o: pl.BlockSpec((GDN_CONV, LANES), lambda b, h: (0, o + h))
    padded = pl.cdiv(seq, GDN_CHUNK) * GDN_CHUNK
    return pl.pallas_call(
        functools.partial(_gdn_kernel, seq=seq),
        grid=(bsz, hh),
        in_specs=[
            spec(off), spec(off + hh), spec(off + 2 * hh), spec(off + 3 * hh),
            pl.BlockSpec((1, seq, LANES), lambda b, h: (b, 0, off + 4 * hh)),
            cspec(0), cspec(hh), cspec(2 * hh),
            pl.BlockSpec((2, LANES), lambda b, h: (0, 0)),
            pl.BlockSpec((1, LANES), lambda b, h: (0, 0)),
        ],
        out_specs=pl.BlockSpec((1, seq, LANES), lambda b, h: (b, 0, h)),
        out_shape=jax.ShapeDtypeStruct((bsz, seq, hh * GDN_DV), BF16),
        scratch_shapes=([pltpu.VMEM((seq + 8, LANES), F32)] + [pltpu.VMEM((padded, LANES), F32)] * 7
                        + [pltpu.VMEM((padded, LANES), BF16)] * 4 + [pltpu.VMEM((8 * padded // GDN_CHUNK, LANES), F32)]),
        compiler_params=_cparams("parallel", "parallel"),
        name="gated_deltanet",
    )(p3d, p3d, p3d, p3d, p3d, conv_w, conv_w, conv_w, gate, g_out)


def _even_mixer(h, bsz, seq, gain, w_in, s5_params, w_glu, g_cq, g_ckv, w_uq, w_ukv, g_q, g_k, w_out):
    d = h.shape[1]
    rope0 = S5_WIDTH + MLA_Q_RANK + MLA_KV_RANK
    w_ext = jnp.concatenate([w_in, _rot_cols(w_in[:, rope0:])], axis=1).astype(BF16)
    p = _norm_matmul(h, gain, w_ext, tn=w_ext.shape[1], name="even_in_proj")
    p3d = p.reshape(bsz, seq, -1)
    y_a = _s5(p3d, _s5_tables(*s5_params))
    wq, wkv, gq, gkn, gkr, cs = _mla_params(w_uq, w_ukv, g_q, g_k, seq)
    y_b = _mla(p3d, g_cq.reshape(1, -1), g_ckv.reshape(1, -1), wq, wkv, gq, gkn, gkr, cs)
    return _out_proj(h, y_a.reshape(-1, S5_WIDTH), y_b.reshape(-1, MLA_HEADS * MLA_V),
                     w_out[:S5_WIDTH].astype(BF16), w_out[S5_WIDTH:].astype(BF16), w_glu.astype(BF16),
                     name="even_out_proj")


def _odd_mixer(h, bsz, seq, gain, w_in, conv_w, a_log, dt_bias, g_out, w_out):
    d = h.shape[1]
    o1 = 3 * SB_W + GDN_QKV
    ab = w_in[:, o1:o1 + 2 * GDN_HEADS]
    w_ext = jnp.concatenate(
        [w_in[:, :o1], w_in[:, o1 + 2 * GDN_HEADS:], ab, jnp.zeros((d, LANES - 2 * GDN_HEADS), w_in.dtype)],
        axis=1).astype(BF16)
    p = _norm_matmul(h, gain, w_ext, tn=w_ext.shape[1] // 3, tm_target=384, name="odd_in_proj")
    p3d = p.reshape(bsz, seq, -1)
    y_c = _stick_breaking(p3d)
    pad = jnp.zeros((LANES - GDN_HEADS,), F32)
    gate = jnp.stack([jnp.concatenate([a_log.astype(F32), pad]), jnp.concatenate([dt_bias.astype(F32), pad])])
    y_d = _gated_deltanet(p3d, conv_w.astype(F32), gate, g_out.reshape(1, -1).astype(F32))
    return _out_proj(h, y_c.reshape(-1, SB_W), y_d.reshape(-1, GDN_HEADS * GDN_DV),
                     w_out[:SB_W].astype(BF16), w_out[SB_W:].astype(BF16), name="odd_out_proj")


def kernel(x, meta_tokens, norm_ffn1, w1_gate, w1_up, w1_down, norm_mix, norm_ffn2, w2_gate, w2_up, w2_down, ev_w_in, s5_log_dt, s5_a_re, s5_a_im, s5_b_re, s5_b_im, s5_c_re, s5_c_im, s5_d, s5_w_glu, mla_g_cq, mla_g_ckv, mla_w_uq, mla_w_ukv, mla_g_q, mla_g_k, ev_w_out, od_w_in, gdn_conv, gdn_a_log, gdn_dt_bias, gdn_g_out, od_w_out):
    bsz, _, d = x.shape
    depth = norm_ffn1.shape[0]
    meta = jnp.broadcast_to(meta_tokens[None].astype(x.dtype), (bsz, N_META, d))
    h3 = jnp.concatenate([meta, x], axis=1)
    seq = h3.shape[1]
    h = h3.reshape(bsz * seq, d)
    for l in range(depth):
        i = l // 2
        h = _ffn(h, norm_ffn1[l], w1_gate[l].astype(BF16), w1_up[l].astype(BF16), w1_down[l].astype(BF16))
        if l % 2 == 0:
            s5_params = (s5_log_dt[i], s5_a_re[i], s5_a_im[i], s5_b_re[i], s5_b_im[i], s5_c_re[i],
                         s5_c_im[i], s5_d[i])
            h = _even_mixer(h, bsz, seq, norm_mix[l], ev_w_in[i], s5_params, s5_w_glu[i], mla_g_cq[i],
                            mla_g_ckv[i], mla_w_uq[i], mla_w_ukv[i], mla_g_q[i], mla_g_k[i], ev_w_out[i])
        else:
            h = _odd_mixer(h, bsz, seq, norm_mix[l], od_w_in[i], gdn_conv[i], gdn_a_log[i], gdn_dt_bias[i],
                           gdn_g_out[i], od_w_out[i])
        h = _ffn(h, norm_ffn2[l], w2_gate[l].astype(BF16), w2_up[l].astype(BF16), w2_down[l].astype(BF16))
    return h.reshape(bsz, seq, d)[:, N_META:]
```

```python
import functools
import math

import jax
import jax.numpy as jnp
from jax import lax
from jax.experimental import pallas as pl
from jax.experimental.pallas import tpu as pltpu

F32 = jnp.float32
BF16 = jnp.bfloat16
EPS = 1e-6
N_META = 16
LANES = 128
VMEM_LIMIT_BYTES = 56 * 2**20
NEG_BIG = -1e30

S5_GROUP, S5_STATE, S5_GROUPS = 16, 64, 32
S5_WIDTH = S5_GROUPS * S5_GROUP
S5_BUNDLE_GROUPS = LANES // S5_GROUP
S5_BUNDLES = S5_GROUPS // S5_BUNDLE_GROUPS
S5_BSTATE = S5_BUNDLE_GROUPS * S5_STATE
S5_CHUNK = 128
MLA_HEADS, MLA_Q_RANK, MLA_KV_RANK = 8, 512, 256
MLA_NOPE, MLA_ROPE, MLA_V = 128, 64, 128
MLA_QK = MLA_NOPE + MLA_ROPE
ROPE_THETA = 10000.0
ATT_BLOCK = 512
SB_BLOCK = 256
ATT_Q_BLOCK = 512
ATT_HEADS_PER_STEP = 2
SB_HEADS, SB_DIM = 8, 128
SB_W = SB_HEADS * SB_DIM
GDN_HEADS, GDN_DK, GDN_DV, GDN_CONV = 8, 128, 128, 4
GDN_CHUNK = 128
GDN_UNROLL = 8
GDN_QKV = GDN_HEADS * (2 * GDN_DK + GDN_DV)


def _cparams(*sem):
    return pltpu.CompilerParams(dimension_semantics=sem, vmem_limit_bytes=VMEM_LIMIT_BYTES)


def _row_tile(rows, target):
    best = None
    for t in range(16, min(rows, target) + 1, 16):
        if rows % t == 0:
            best = t
    assert best is not None, rows
    return best


def _rmsnorm(x, g):
    return x * lax.rsqrt(jnp.mean(x * x, axis=-1, keepdims=True) + EPS) * g


def _sigmoid(x):
    return 1.0 / (1.0 + jnp.exp(-x))


def _dot(a, b):
    return jnp.dot(a, b, preferred_element_type=F32)


def _aligned(x, m):
    return x if isinstance(x, int) else pl.multiple_of(x, m)


def _dot_nt(a, b):
    return lax.dot_general(a, b, (((1,), (1,)), ((), ())), preferred_element_type=F32)


def _ffn_kernel(x_ref, g_ref, wg_ref, wu_ref, wd_ref, o_ref, xn_ref):
    j = pl.program_id(1)

    @pl.when(j == 0)
    def _():
        xn_ref[...] = _rmsnorm(x_ref[...], g_ref[...]).astype(BF16)
        o_ref[...] = jnp.zeros_like(o_ref)

    xn = xn_ref[...]
    gate = _dot(xn, wg_ref[...])
    up = _dot(xn, wu_ref[...])
    act = (gate * _sigmoid(gate) * up).astype(BF16)
    o_ref[...] += _dot(act, wd_ref[...])

    @pl.when(j == pl.num_programs(1) - 1)
    def _():
        o_ref[...] = x_ref[...] + 0.5 * o_ref[...]


def _ffn(h, gain, wg, wu, wd, *, tm_target=768, tf=512):
    rows, d = h.shape
    f = wg.shape[1]
    tm = _row_tile(rows, tm_target)
    assert f % tf == 0
    return pl.pallas_call(
        _ffn_kernel,
        grid=(rows // tm, f // tf),
        in_specs=[
            pl.BlockSpec((tm, d), lambda i, j: (i, 0)),
            pl.BlockSpec((1, d), lambda i, j: (0, 0)),
            pl.BlockSpec((d, tf), lambda i, j: (0, j)),
            pl.BlockSpec((d, tf), lambda i, j: (0, j)),
            pl.BlockSpec((tf, d), lambda i, j: (j, 0)),
        ],
        out_specs=pl.BlockSpec((tm, d), lambda i, j: (i, 0)),
        out_shape=jax.ShapeDtypeStruct((rows, d), F32),
        scratch_shapes=[pltpu.VMEM((tm, d), BF16)],
        compiler_params=_cparams("parallel", "arbitrary"),
        name="ffn",
    )(h, gain.reshape(1, d), wg, wu, wd)


def _norm_matmul_kernel(x_ref, g_ref, w_ref, o_ref):
    xn = _rmsnorm(x_ref[...], g_ref[...]).astype(BF16)
    o_ref[...] = _dot(xn, w_ref[...]).astype(o_ref.dtype)


def _norm_matmul(h, gain, w, *, tn, tm_target=768, name="norm_matmul"):
    rows, d = h.shape
    n = w.shape[1]
    tm = _row_tile(rows, tm_target)
    assert n % tn == 0
    return pl.pallas_call(
        _norm_matmul_kernel,
        grid=(n // tn, rows // tm),
        in_specs=[
            pl.BlockSpec((tm, d), lambda j, i: (i, 0)),
            pl.BlockSpec((1, d), lambda j, i: (0, 0)),
            pl.BlockSpec((d, tn), lambda j, i: (0, j), pipeline_mode=pl.Buffered(1)),
        ],
        out_specs=pl.BlockSpec((tm, tn), lambda j, i: (i, j)),
        out_shape=jax.ShapeDtypeStruct((rows, n), F32),
        compiler_params=_cparams("parallel", "parallel"),
        name=name,
    )(h, gain.reshape(1, d), w)


def _out_proj_kernel(h_ref, ya_ref, yb_ref, wa_ref, wb_ref, *rest, glu):
    if glu:
        wglu_ref, o_ref = rest
        y = ya_ref[...]
        gate = _dot(y, wglu_ref[...])
        ya = (y.astype(F32) * _sigmoid(gate)).astype(BF16)
    else:
        (o_ref,) = rest
        ya = ya_ref[...]
    o_ref[...] = h_ref[...] + _dot(ya, wa_ref[...]) + _dot(yb_ref[...], wb_ref[...])


def _out_proj(h, ya, yb, wa, wb, wglu=None, *, tm_target=768, name="out_proj"):
    rows, d = h.shape
    ka, kb = ya.shape[1], yb.shape[1]
    tm = _row_tile(rows, tm_target)
    glu = wglu is not None
    in_specs = [
        pl.BlockSpec((tm, d), lambda i: (i, 0)),
        pl.BlockSpec((tm, ka), lambda i: (i, 0)),
        pl.BlockSpec((tm, kb), lambda i: (i, 0)),
        pl.BlockSpec((ka, d), lambda i: (0, 0)),
        pl.BlockSpec((kb, d), lambda i: (0, 0)),
    ]
    args = [h, ya, yb, wa, wb]
    if glu:
        in_specs.append(pl.BlockSpec((ka, ka), lambda i: (0, 0)))
        args.append(wglu)
    return pl.pallas_call(
        functools.partial(_out_proj_kernel, glu=glu),
        grid=(rows // tm,),
        in_specs=in_specs,
        out_specs=pl.BlockSpec((tm, d), lambda i: (i, 0)),
        out_shape=jax.ShapeDtypeStruct((rows, d), F32),
        compiler_params=_cparams("parallel"),
        name=name,
    )(*args)


def _cmul(ar, ai, br, bi):
    return ar * br - ai * bi, ar * bi + ai * br


def _s5_kernel(u_ref, bb_ref, cm_ref, tneg_ref, tpos_ref, tpos1_ref, d_ref, o_ref, x_s, s_s, *, seq):
    n = S5_BSTATE
    c = S5_CHUNK
    u = u_ref[0]
    x_s[...] = _dot(u.astype(BF16), bb_ref[0])

    def tri(size):
        r = lax.broadcasted_iota(jnp.int32, (size, size), 0)
        col = lax.broadcasted_iota(jnp.int32, (size, size), 1)
        return (r >= col).astype(BF16)

    def chunk(r0, size, ltri, carry):
        cr, ci = carry
        x = x_s[pl.ds(r0, size), :]
        tn = tneg_ref[0, pl.ds(0, size), :]
        tp = tpos_ref[0, pl.ds(0, size), :]
        tq = tpos1_ref[0, pl.ds(0, size), :]
        xr, xi = _cmul(x[:, :n], x[:, n:], tn[:, :n], tn[:, n:])
        acc = _dot(ltri, jnp.concatenate([xr, xi], axis=1).astype(BF16))
        sr, si = _cmul(acc[:, :n], acc[:, n:], tp[:, :n], tp[:, n:])
        qr, qi = _cmul(tq[:, :n], tq[:, n:], cr, ci)
        sr = sr + qr
        si = si + qi
        s_s[pl.ds(r0, size), :] = jnp.concatenate([sr, si], axis=1).astype(BF16)
        return sr[size - 1:size, :], si[size - 1:size, :]

    nfull = seq // c
    tail = seq - nfull * c
    ltri = tri(c)
    zero = jnp.zeros((1, n), F32)
    carry = lax.fori_loop(
        0, nfull, lambda i, cy: chunk(pl.multiple_of(i * c, c), c, ltri, cy), (zero, zero))
    if tail:
        chunk(nfull * c, tail, tri(tail), carry)
    y = _dot(s_s[...], cm_ref[0]) + d_ref[0] * u
    o_ref[0] = jax.nn.gelu(y, approximate=True).astype(o_ref.dtype)


def _s5(p3d, tabs):
    bsz, seq, _ = p3d.shape
    n2 = 2 * S5_BSTATE
    tab_spec = pl.BlockSpec((1, S5_CHUNK, n2), lambda b, k: (k, 0, 0))
    return pl.pallas_call(
        functools.partial(_s5_kernel, seq=seq),
        grid=(bsz, S5_BUNDLES),
        in_specs=[
            pl.BlockSpec((1, seq, LANES), lambda b, k: (b, 0, k)),
            pl.BlockSpec((1, LANES, n2), lambda b, k: (k, 0, 0)),
            pl.BlockSpec((1, n2, LANES), lambda b, k: (k, 0, 0)),
            tab_spec, tab_spec, tab_spec,
            pl.BlockSpec((1, 1, LANES), lambda b, k: (k, 0, 0)),
        ],
        out_specs=pl.BlockSpec((1, seq, LANES), lambda b, k: (b, 0, k)),
        out_shape=jax.ShapeDtypeStruct((bsz, seq, S5_WIDTH), BF16),
        scratch_shapes=[pltpu.VMEM((seq, n2), F32), pltpu.VMEM((seq, n2), BF16)],
        compiler_params=_cparams("parallel", "parallel"),
        name="s5",
    )(p3d, tabs["bbar"], tabs["cmat"], tabs["tneg"], tabs["tpos"], tabs["tpos1"], tabs["dskip"])


def _s5_tables(log_dt, a_re, a_im, b_re, b_im, c_re, c_im, d_skip):
    g, p, c = S5_GROUPS, S5_STATE, S5_GROUP
    nb, gb = S5_BUNDLES, S5_BUNDLE_GROUPS
    dt = jnp.exp(log_dt.astype(F32))[:, None]
    ar, ai = a_re.astype(F32), a_im.astype(F32)
    mag = jnp.exp(dt * ar)
    lam_r, lam_i = mag * jnp.cos(dt * ai), mag * jnp.sin(dt * ai)
    den = ar * ar + ai * ai
    coef_r = ((lam_r - 1.0) * ar + lam_i * ai) / den
    coef_i = (lam_i * ar - (lam_r - 1.0) * ai) / den
    br, bi = b_re.astype(F32), b_im.astype(F32)
    bbar_r = coef_r[..., None] * br - coef_i[..., None] * bi
    bbar_i = coef_r[..., None] * bi + coef_i[..., None] * br
    eye = jnp.eye(gb, dtype=F32)

    def block_in(m):
        m = m.reshape(nb, gb, p, c)
        return jnp.einsum("kgpc,gh->kgchp", m, eye).reshape(nb, gb * c, gb * p)

    def block_out(m):
        m = m.reshape(nb, gb, c, p)
        return jnp.einsum("kgcp,gh->kgphc", m, eye).reshape(nb, gb * p, gb * c)

    bbar = jnp.concatenate([block_in(bbar_r), block_in(bbar_i)], axis=2).astype(BF16)
    cmat = jnp.concatenate([block_out(c_re.astype(F32)), -block_out(c_im.astype(F32))], axis=1).astype(BF16)

    def power_table(offset, sign):
        e = sign * (jnp.arange(S5_CHUNK, dtype=F32) + offset)[:, None, None]
        m = jnp.exp(e * (dt * ar)[None])
        re = (m * jnp.cos(e * (dt * ai)[None])).reshape(S5_CHUNK, nb, gb * p)
        im = (m * jnp.sin(e * (dt * ai)[None])).reshape(S5_CHUNK, nb, gb * p)
        return jnp.transpose(jnp.concatenate([re, im], axis=2), (1, 0, 2))

    return {
        "bbar": bbar, "cmat": cmat,
        "tneg": power_table(0.0, -1.0), "tpos": power_table(0.0, 1.0), "tpos1": power_table(1.0, 1.0),
        "dskip": d_skip.astype(F32).reshape(nb, 1, gb * c),
    }


def _query_blocks(seq, qb, kb):
    assert qb % kb == 0 and seq >= qb and seq % qb <= kb
    nloop = seq // qb - 1
    last_r0 = nloop * qb
    last_rows = seq - last_r0
    diag = [(d * kb, kb) for d in range(qb // kb)]
    last_diag = diag + ([(qb, last_rows - qb)] if last_rows > qb else [])
    return nloop, last_r0, last_rows, diag, last_diag


def _softmax_steps(ss, vs, carries):
    ms, ls, accs = zip(*carries)
    m_news = [jnp.maximum(m, jnp.max(s, axis=-1, keepdims=True)) for m, s in zip(ms, ss)]
    ps = [jnp.exp(s - mn) for s, mn in zip(ss, m_news)]
    alphas = [jnp.exp(m - mn) for m, mn in zip(ms, m_news)]
    ls = [a * l + jnp.sum(p, axis=-1, keepdims=True) for a, l, p in zip(alphas, ls, ps)]
    pvs = [_dot(p.astype(BF16), v) for p, v in zip(ps, vs)]
    accs = [a * acc + pv for a, acc, pv in zip(alphas, accs, pvs)]
    return tuple(zip(m_news, ls, accs))


def _mla_kernel(cq_ref, ckv_ref, kr_ref, gcq_ref, gckv_ref, wq_ref, wkv_ref, gq_ref, gkn_ref, gkr_ref,
                cs_ref, o_ref, cqn_s, ckvn_s, krope_s, krss_s, q_s, k_s, v_s, *, seq):
    qb, kb = ATT_Q_BLOCK, ATT_BLOCK
    heads = range(q_s.shape[0])
    lane = lax.broadcasted_iota(jnp.int32, (seq, LANES), 1)
    first_half = lane < MLA_ROPE
    cs = cs_ref[...]

    def rope_part(t):
        t = t * cs
        t = t + pltpu.roll(t, MLA_ROPE, axis=1)
        return jnp.where(first_half, t, 0.0)

    @pl.when(pl.program_id(1) == 0)
    def _():
        cqn_s[...] = _rmsnorm(cq_ref[0], gcq_ref[...]).astype(BF16)
        ckvn_s[...] = _rmsnorm(ckv_ref[0], gckv_ref[...]).astype(BF16)
        kr = kr_ref[0]
        krope_s[...] = rope_part(kr * gkr_ref[...])
        ss = jnp.sum(jnp.where(first_half, kr * kr, 0.0), axis=-1, keepdims=True)
        krss_s[...] = jnp.broadcast_to(ss, (seq, LANES))

    gq = gq_ref[...]
    xqs = [_dot(cqn_s[...], wq_ref[h]) for h in heads]
    sqs = [lax.rsqrt(jnp.sum(x[:, :LANES] * x[:, :LANES] + jnp.where(first_half, x[:, LANES:] * x[:, LANES:], 0.0),
                             axis=-1, keepdims=True) * (1.0 / MLA_QK) + EPS) * (MLA_QK ** -0.5) for x in xqs]
    for h, x, sq in zip(heads, xqs, sqs):
        q_s[h] = jnp.concatenate(
            [x[:, :LANES] * (sq * gq[:, :LANES]), rope_part(x[:, LANES:] * (sq * gq[:, LANES:]))],
            axis=1).astype(BF16)
    xkvs = [_dot(ckvn_s[...], wkv_ref[h]) for h in heads]
    sks = [lax.rsqrt((jnp.sum(x[:, :LANES] * x[:, :LANES], axis=-1, keepdims=True) + krss_s[...])
                     * (1.0 / MLA_QK) + EPS) for x in xkvs]
    for h, x, sk in zip(heads, xkvs, sks):
        k_s[h] = jnp.concatenate([x[:, :LANES] * (sk * gkn_ref[...]), krope_s[...] * sk], axis=1).astype(BF16)
        v_s[h] = x[:, LANES:].astype(BF16)

    def q_block(r0, rows, nprev, diag):
        qs = [q_s[h, pl.ds(r0, rows), :] for h in heads]

        def scores(c0, size):
            return ([_dot_nt(q, k_s[h, pl.ds(c0, size), :]) for h, q in zip(heads, qs)],
                    [v_s[h, pl.ds(c0, size), :] for h in heads])

        def body(j, carry):
            return _softmax_steps(*scores(pl.multiple_of(j * kb, kb), kb), carry)

        init = (jnp.full((rows, 1), NEG_BIG, F32), jnp.zeros((rows, 1), F32), jnp.zeros((rows, MLA_V), F32))
        carry = lax.fori_loop(0, nprev, body, tuple(init for _ in heads))
        for off, size in diag:
            ss, vs = scores(r0 + off, size)
            r = lax.broadcasted_iota(jnp.int32, (rows, size), 0)
            c = lax.broadcasted_iota(jnp.int32, (rows, size), 1)
            carry = _softmax_steps([jnp.where(c + off <= r, s, NEG_BIG) for s in ss], vs, carry)
        for h, (_, l, acc) in zip(heads, carry):
            o_ref[0, pl.ds(r0, rows), h * MLA_V:(h + 1) * MLA_V] = (acc / l).astype(o_ref.dtype)

    nloop, last_r0, last_rows, diag, last_diag = _query_blocks(seq, qb, kb)

    def outer(i, _):
        q_block(pl.multiple_of(i * qb, qb), qb, i * (qb // kb), diag)
        return 0

    lax.fori_loop(0, nloop, outer, 0)
    q_block(last_r0, last_rows, last_r0 // kb, last_diag)


def _mla(p3d, g_cq, g_ckv, wq, wkv, gq, gkn, gkr, cs):
    bsz, seq, _ = p3d.shape
    hp = ATT_HEADS_PER_STEP
    full = lambda shape: pl.BlockSpec(shape, lambda b, g: (0,) * len(shape))
    return pl.pallas_call(
        functools.partial(_mla_kernel, seq=seq),
        grid=(bsz, MLA_HEADS // hp),
        in_specs=[
            pl.BlockSpec((1, seq, MLA_Q_RANK), lambda b, g: (b, 0, S5_WIDTH // MLA_Q_RANK)),
            pl.BlockSpec((1, seq, MLA_KV_RANK), lambda b, g: (b, 0, (S5_WIDTH + MLA_Q_RANK) // MLA_KV_RANK)),
            pl.BlockSpec((1, seq, LANES), lambda b, g: (b, 0, (S5_WIDTH + MLA_Q_RANK + MLA_KV_RANK) // LANES)),
            full((1, MLA_Q_RANK)), full((1, MLA_KV_RANK)),
            pl.BlockSpec((hp, MLA_Q_RANK, 2 * LANES), lambda b, g: (g, 0, 0)),
            pl.BlockSpec((hp, MLA_KV_RANK, 2 * LANES), lambda b, g: (g, 0, 0)),
            full((1, 2 * LANES)), full((1, LANES)), full((1, LANES)),
            full((seq, LANES)),
        ],
        out_specs=pl.BlockSpec((1, seq, hp * MLA_V), lambda b, g: (b, 0, g)),
        out_shape=jax.ShapeDtypeStruct((bsz, seq, MLA_HEADS * MLA_V), BF16),
        scratch_shapes=[pltpu.VMEM((seq, MLA_Q_RANK), BF16), pltpu.VMEM((seq, MLA_KV_RANK), BF16),
                        pltpu.VMEM((seq, LANES), F32), pltpu.VMEM((seq, LANES), F32),
                        pltpu.VMEM((hp, seq, 2 * LANES), BF16), pltpu.VMEM((hp, seq, 2 * LANES), BF16),
                        pltpu.VMEM((hp, seq, MLA_V), BF16)],
        compiler_params=_cparams("parallel", "arbitrary"),
        name="mla",
    )(p3d, p3d, p3d, g_cq, g_ckv, wq, wkv, gq, gkn, gkr, cs)


def _rot_cols(w):
    half = MLA_ROPE // 2
    return jnp.concatenate([-w[..., half:], w[..., :half]], axis=-1)


def _swap_halves(g):
    half = MLA_ROPE // 2
    return jnp.concatenate([g[..., half:], g[..., :half]], axis=-1)


def _mla_params(w_uq, w_ukv, g_q, g_k, seq):
    wq = w_uq.reshape(MLA_Q_RANK, MLA_HEADS, MLA_QK)
    wq = jnp.concatenate([wq, _rot_cols(wq[..., MLA_NOPE:])], axis=-1)
    wq = jnp.transpose(wq, (1, 0, 2)).astype(BF16)
    wkv = jnp.transpose(w_ukv.reshape(MLA_KV_RANK, MLA_HEADS, MLA_NOPE + MLA_V), (1, 0, 2)).astype(BF16)
    gq = jnp.concatenate([g_q, _swap_halves(g_q[MLA_NOPE:])]).reshape(1, 2 * LANES).astype(F32)
    gkn = g_k[:MLA_NOPE].reshape(1, LANES).astype(F32)
    gkr = jnp.concatenate([g_k[MLA_NOPE:], _swap_halves(g_k[MLA_NOPE:])]).reshape(1, LANES).astype(F32)
    half = MLA_ROPE // 2
    inv = ROPE_THETA ** (-jnp.arange(half, dtype=F32) / half)
    ang = jnp.arange(seq, dtype=F32)[:, None] * inv[None, :]
    cs = jnp.concatenate([jnp.cos(ang), jnp.cos(ang), jnp.sin(ang), jnp.sin(ang)], axis=1)
    return wq, wkv, gq, gkn, gkr, cs


def _split2(x):
    hi = x.astype(BF16)
    return hi, (x - hi.astype(F32)).astype(BF16)


def _split3(x):
    hi = x.astype(BF16)
    r = x - hi.astype(F32)
    mid = r.astype(BF16)
    return hi, mid, (r - mid.astype(F32)).astype(BF16)


def _sb_kernel(q_ref, k_ref, v_ref, o_ref, q_s, k_s, v_s, *, seq):
    qb, blk = ATT_Q_BLOCK, SB_BLOCK
    heads = range(q_s.shape[0])
    for h in heads:
        cols = slice(h * SB_DIM, (h + 1) * SB_DIM)
        q_s[h] = (q_ref[0, :, cols] * (SB_DIM ** -0.5)).astype(BF16)
        k_s[h] = k_ref[0, :, cols].astype(BF16)
        v_s[h] = v_ref[0, :, cols].astype(BF16)

    def later_keys(size):
        r = lax.broadcasted_iota(jnp.int32, (size, size), 0)
        c = lax.broadcasted_iota(jnp.int32, (size, size), 1)
        return (r > c).astype(BF16)

    def tiles(qs, c0, size, u_mat, strict, carries):
        tails, accs = zip(*carries)
        zs = [_dot_nt(q, k_s[h, pl.ds(c0, size), :]) for h, q in zip(heads, qs)]
        log_keeps = [-(jnp.maximum(z, 0.0) + jnp.log(1.0 + jnp.exp(-jnp.abs(z)))) for z in zs]
        if strict is not None:
            log_keeps = [jnp.where(strict, lk, 0.0) for lk in log_keeps]
        splits = [_split2(lk) for lk in log_keeps]
        inners = [_dot(hi, u_mat) + _dot(lo, u_mat) for hi, lo in splits]
        ws = [jnp.exp(z + lk + inner + tail) for z, lk, inner, tail in zip(zs, log_keeps, inners, tails)]
        if strict is not None:
            ws = [jnp.where(strict, w, 0.0) for w in ws]
        accs = [acc + _dot(w.astype(BF16), v_s[h, pl.ds(c0, size), :]) for h, w, acc in zip(heads, ws, accs)]
        tails = [tail + jnp.sum(lk, axis=-1, keepdims=True) for tail, lk in zip(tails, log_keeps)]
        return tuple(zip(tails, accs))

    u_blk = later_keys(blk)

    def q_block(r0, rows, nprev, diag):
        qs = [q_s[h, pl.ds(r0, rows), :] for h in heads]
        carry = tuple((jnp.zeros((rows, 1), F32), jnp.zeros((rows, SB_DIM), F32)) for _ in heads)
        for off, size, u_mat in reversed(diag):
            r = lax.broadcasted_iota(jnp.int32, (rows, size), 0)
            c = lax.broadcasted_iota(jnp.int32, (rows, size), 1)
            carry = tiles(qs, r0 + off, size, u_mat, c + off < r, carry)

        def body(jj, cy):
            c0 = pl.multiple_of((nprev - 1 - jj) * blk, blk)
            return tiles(qs, c0, blk, u_blk, None, cy)

        carry = lax.fori_loop(0, nprev, body, carry)
        for h, (_, acc) in zip(heads, carry):
            o_ref[0, pl.ds(r0, rows), h * SB_DIM:(h + 1) * SB_DIM] = acc.astype(o_ref.dtype)

    nloop, last_r0, last_rows, diag, last_diag = _query_blocks(seq, qb, blk)
    u_of = lambda size: u_blk if size == blk else later_keys(size)

    def outer(i, _):
        q_block(pl.multiple_of(i * qb, qb), qb, i * (qb // blk), [(o, s, u_of(s)) for o, s in diag])
        return 0

    lax.fori_loop(0, nloop, outer, 0)
    q_block(last_r0, last_rows, last_r0 // blk, [(o, s, u_of(s)) for o, s in last_diag])


def _stick_breaking(p3d):
    bsz, seq, _ = p3d.shape
    hp = ATT_HEADS_PER_STEP
    groups = SB_HEADS // hp
    spec = lambda off: pl.BlockSpec((1, seq, hp * SB_DIM), lambda b, g: (b, 0, off + g))
    return pl.pallas_call(
        functools.partial(_sb_kernel, seq=seq),
        grid=(bsz, groups),
        in_specs=[spec(0), spec(groups), spec(2 * groups)],
        out_specs=pl.BlockSpec((1, seq, hp * SB_DIM), lambda b, g: (b, 0, g)),
        out_shape=jax.ShapeDtypeStruct((bsz, seq, SB_W), BF16),
        scratch_shapes=[pltpu.VMEM((hp, seq, SB_DIM), BF16)] * 3,
        compiler_params=_cparams("parallel", "parallel"),
        name="stick_breaking",
    )(p3d, p3d, p3d)


def _gdn_kernel(q_ref, k_ref, v_ref, z_ref, ab_ref, cwq_ref, cwk_ref, cwv_ref, gate_ref, gout_ref, o_ref,
                pad_s, q_s, k_s, v_s, g_s, b_s, o_s, u_s, w_s, a_s, qd_s, kdt_s, gl_s, *, seq):
    c = GDN_CHUNK
    h = pl.program_id(1)
    nchunks = pl.cdiv(seq, c)
    padded = nchunks * c

    def conv_silu(x_ref, w_ref):
        pad_s[pl.ds(0, 8), :] = jnp.zeros((8, LANES), F32)
        pad_s[pl.ds(8, seq), :] = x_ref[0]
        w = w_ref[...]
        y = w[GDN_CONV - 1:GDN_CONV, :] * pad_s[pl.ds(8, seq), :]
        for j in range(GDN_CONV - 1):
            y = y + w[j:j + 1, :] * pad_s[pl.ds(8 - (GDN_CONV - 1) + j, seq), :]
        return y * _sigmoid(y)

    def l2norm(x):
        return x * lax.rsqrt(jnp.sum(x * x, axis=-1, keepdims=True) + EPS)

    zeros_tail = jnp.zeros((padded - seq, LANES), F32)
    for ref in (q_s, k_s, v_s, g_s, b_s):
        if padded > seq:
            ref[pl.ds(seq, padded - seq), :] = zeros_tail
    q_s[pl.ds(0, seq), :] = l2norm(conv_silu(q_ref, cwq_ref)) * (GDN_DK ** -0.5)
    k_s[pl.ds(0, seq), :] = l2norm(conv_silu(k_ref, cwk_ref))
    v_s[pl.ds(0, seq), :] = conv_silu(v_ref, cwv_ref)

    ab = ab_ref[0]
    lane = lax.broadcasted_iota(jnp.int32, (seq, LANES), 1)
    gate = gate_ref[...]
    t = ab + gate[1:2, :]
    g_all = -jnp.exp(gate[0:1, :]) * (jnp.maximum(t, 0.0) + jnp.log(1.0 + jnp.exp(-jnp.abs(t))))
    g_col = jnp.sum(jnp.where(lane == h, g_all, 0.0), axis=-1, keepdims=True)
    b_col = jnp.sum(jnp.where(lane == h + GDN_HEADS, _sigmoid(ab), 0.0), axis=-1, keepdims=True)
    g_s[pl.ds(0, seq), :] = jnp.broadcast_to(g_col, (seq, LANES))
    b_s[pl.ds(0, seq), :] = jnp.broadcast_to(b_col, (seq, LANES))

    r = lax.broadcasted_iota(jnp.int32, (c, c), 0)
    col = lax.broadcasted_iota(jnp.int32, (c, c), 1)
    lower = r >= col
    strict = r > col
    incl = lower.astype(BF16)
    eye = (r == col).astype(F32)
    level_masks = [((r >> l) == (col >> l)) & ((r >> (l - 1)) != (col >> (l - 1))) & strict
                   for l in range(1, int(math.log2(c)) + 1)]

    def local(first, count):
        ids = [first + t for t in range(count)]
        rows = [pl.ds(_aligned(i * c, c), c) for i in ids]
        ks = [k_s[r, :] for r in rows]
        betas = [b_s[r, :] for r in rows]
        gbs = [g_s[r, :] for r in rows]
        sums = [_dot(incl, jnp.concatenate(_split3(gb) + _split3(jnp.where(strict, gb, 0.0)), axis=1))
                for gb in gbs]
        cums = [s[:, 0:c] + s[:, c:2 * c] + s[:, 2 * c:3 * c] for s in sums]
        decays = [jnp.exp(jnp.where(lower, s[:, 3 * c:4 * c] + s[:, 4 * c:5 * c] + s[:, 5 * c:6 * c], NEG_BIG))
                  for s in sums]
        kbs = [k * beta for k, beta in zip(ks, betas)]
        kbfs = [k.astype(BF16) for k in ks]
        ms = [jnp.where(strict, _dot_nt(kb.astype(BF16), kbf) * decay, 0.0)
              for kb, kbf, decay in zip(kbs, kbfs, decays)]
        invs = [eye.astype(BF16)] * count
        for mask in level_masks:
            ts = [_dot(jnp.where(mask, m, 0.0).astype(BF16), inv).astype(BF16) for m, inv in zip(ms, invs)]
            invs = [(inv.astype(F32) - _dot(inv, t)).astype(BF16) for inv, t in zip(invs, ts)]
        e_cums = [jnp.exp(cum) for cum in cums]
        for r, inv, kb, e_cum in zip(rows, invs, kbs, e_cums):
            w_s[r, :] = _dot(inv, (kb * e_cum).astype(BF16)).astype(BF16)
        for r, inv, beta in zip(rows, invs, betas):
            u_s[r, :] = _dot(inv, (v_s[r, :] * beta).astype(BF16))
        qs = [q_s[r, :] for r in rows]
        for r, q, kbf, decay in zip(rows, qs, kbfs, decays):
            a_s[r, :] = jnp.where(lower, _dot_nt(q.astype(BF16), kbf) * decay, 0.0).astype(BF16)
        for i, r, q, k, cum, e_cum in zip(ids, rows, qs, ks, cums, e_cums):
            qd_s[r, :] = (q * e_cum).astype(BF16)
            last = cum[c - 1:c, :]
            kdt_s[r, :] = (k * jnp.exp(last - cum)).T.astype(BF16)
            gl_s[pl.ds(_aligned(i * 8, 8), 8), :] = jnp.broadcast_to(jnp.exp(last), (8, LANES))

    def recur(i, state):
        r0 = pl.multiple_of(i * c, c)
        sb = state.astype(BF16)
        v_new = (u_s[pl.ds(r0, c), :] - _dot(w_s[pl.ds(r0, c), :], sb)).astype(BF16)
        o_s[pl.ds(r0, c), :] = _dot(qd_s[pl.ds(r0, c), :], sb) + _dot(a_s[pl.ds(r0, c), :], v_new)
        g_last = gl_s[pl.ds(pl.multiple_of(i * 8, 8), 1), :]
        return state * g_last + _dot(kdt_s[pl.ds(r0, c), :], v_new)

    ngroups = max(nchunks // GDN_UNROLL, 1)
    for gi in range(ngroups):
        first = gi * GDN_UNROLL
        local(first, GDN_UNROLL if gi < ngroups - 1 else nchunks - first)
    lax.fori_loop(0, nchunks, recur, jnp.zeros((GDN_DK, GDN_DV), F32))
    o = o_s[pl.ds(0, seq), :]
    z = z_ref[0]
    o_ref[0] = (_rmsnorm(o, gout_ref[...]) * (z * _sigmoid(z))).astype(o_ref.dtype)


def _gated_deltanet(p3d, conv_w, gate, g_out):
    bsz, seq, _ = p3d.shape
    hh = GDN_HEADS
    off = 3 * SB_HEADS
    spec = lambda o: pl.BlockSpec((1, seq, LANES), lambda b, h: (b, 0, o + h))
    cspec = lambda o: pl.BlockSpec((GDN_CONV, LANES), lambda b, h: (0, o + h))
    padded = pl.cdiv(seq, GDN_CHUNK) * GDN_CHUNK
    return pl.pallas_call(
        functools.partial(_gdn_kernel, seq=seq),
        grid=(bsz, hh),
        in_specs=[
            spec(off), spec(off + hh), spec(off + 2 * hh), spec(off + 3 * hh),
            pl.BlockSpec((1, seq, LANES), lambda b, h: (b, 0, off + 4 * hh)),
            cspec(0), cspec(hh), cspec(2 * hh),
            pl.BlockSpec((2, LANES), lambda b, h: (0, 0)),
            pl.BlockSpec((1, LANES), lambda b, h: (0, 0)),
        ],
        out_specs=pl.BlockSpec((1, seq, LANES), lambda b, h: (b, 0, h)),
        out_shape=jax.ShapeDtypeStruct((bsz, seq, hh * GDN_DV), BF16),
        scratch_shapes=([pltpu.VMEM((seq + 8, LANES), F32)] + [pltpu.VMEM((padded, LANES), F32)] * 7
                        + [pltpu.VMEM((padded, LANES), BF16)] * 4 + [pltpu.VMEM((8 * padded // GDN_CHUNK, LANES), F32)]),
        compiler_params=_cparams("parallel", "parallel"),
        name="gated_deltanet",
    )(p3d, p3d, p3d, p3d, p3d, conv_w, conv_w, conv_w, gate, g_out)


def _even_mixer(h, bsz, seq, gain, w_in, s5_params, w_glu, g_cq, g_ckv, w_uq, w_ukv, g_q, g_k, w_out):
    d = h.shape[1]
    rope0 = S5_WIDTH + MLA_Q_RANK + MLA_KV_RANK
    w_ext = jnp.concatenate([w_in, _rot_cols(w_in[:, rope0:])], axis=1).astype(BF16)
    p = _norm_matmul(h, gain, w_ext, tn=w_ext.shape[1], name="even_in_proj")
    p3d = p.reshape(bsz, seq, -1)
    y_a = _s5(p3d, _s5_tables(*s5_params))
    wq, wkv, gq, gkn, gkr, cs = _mla_params(w_uq, w_ukv, g_q, g_k, seq)
    y_b = _mla(p3d, g_cq.reshape(1, -1), g_ckv.reshape(1, -1), wq, wkv, gq, gkn, gkr, cs)
    return _out_proj(h, y_a.reshape(-1, S5_WIDTH), y_b.reshape(-1, MLA_HEADS * MLA_V),
                     w_out[:S5_WIDTH].astype(BF16), w_out[S5_WIDTH:].astype(BF16), w_glu.astype(BF16),
                     name="even_out_proj")


def _odd_mixer(h, bsz, seq, gain, w_in, conv_w, a_log, dt_bias, g_out, w_out):
    d = h.shape[1]
    o1 = 3 * SB_W + GDN_QKV
    ab = w_in[:, o1:o1 + 2 * GDN_HEADS]
    w_ext = jnp.concatenate(
        [w_in[:, :o1], w_in[:, o1 + 2 * GDN_HEADS:], ab, jnp.zeros((d, LANES - 2 * GDN_HEADS), w_in.dtype)],
        axis=1).astype(BF16)
    p = _norm_matmul(h, gain, w_ext, tn=w_ext.shape[1] // 3, name="odd_in_proj")
    p3d = p.reshape(bsz, seq, -1)
    y_c = _stick_breaking(p3d)
    pad = jnp.zeros((LANES - GDN_HEADS,), F32)
    gate = jnp.stack([jnp.concatenate([a_log.astype(F32), pad]), jnp.concatenate([dt_bias.astype(F32), pad])])
    y_d = _gated_deltanet(p3d, conv_w.astype(F32), gate, g_out.reshape(1, -1).astype(F32))
    return _out_proj(h, y_c.reshape(-1, SB_W), y_d.reshape(-1, GDN_HEADS * GDN_DV),
                     w_out[:SB_W].astype(BF16), w_out[SB_W:].astype(BF16), name="odd_out_proj")


def kernel(x, meta_tokens, norm_ffn1, w1_gate, w1_up, w1_down, norm_mix, norm_ffn2, w2_gate, w2_up, w2_down, ev_w_in, s5_log_dt, s5_a_re, s5_a_im, s5_b_re, s5_b_im, s5_c_re, s5_c_im, s5_d, s5_w_glu, mla_g_cq, mla_g_ckv, mla_w_uq, mla_w_ukv, mla_g_q, mla_g_k, ev_w_out, od_w_in, gdn_conv, gdn_a_log, gdn_dt_bias, gdn_g_out, od_w_out):
    bsz, _, d = x.shape
    depth = norm_ffn1.shape[0]
    meta = jnp.broadcast_to(meta_tokens[None].astype(x.dtype), (bsz, N_META, d))
    h3 = jnp.concatenate([meta, x], axis=1)
    seq = h3.shape[1]
    h = h3.reshape(bsz * seq, d)
    for l in range(depth):
        i = l // 2
        h = _ffn(h, norm_ffn1[l], w1_gate[l].astype(BF16), w1_up[l].astype(BF16), w1_down[l].astype(BF16))
        if l % 2 == 0:
            s5_params = (s5_log_dt[i], s5_a_re[i], s5_a_im[i], s5_b_re[i], s5_b_im[i], s5_c_re[i],
                         s5_c_im[i], s5_d[i])
            h = _even_mixer(h, bsz, seq, norm_mix[l], ev_w_in[i], s5_params, s5_w_glu[i], mla_g_cq[i],
                            mla_g_ckv[i], mla_w_uq[i], mla_w_ukv[i], mla_g_q[i], mla_g_k[i], ev_w_out[i])
        else:
            h = _odd_mixer(h, bsz, seq, norm_mix[l], od_w_in[i], gdn_conv[i], gdn_a_log[i], gdn_dt_bias[i],
                           gdn_g_out[i], od_w_out[i])
        h = _ffn(h, norm_ffn2[l], w2_gate[l].astype(BF16), w2_up[l].astype(BF16), w2_down[l].astype(BF16))
    return h.reshape(bsz, seq, d)[:, N_META:]
```

```python
import functools
import math

import jax
import jax.numpy as jnp
from jax import lax
from jax.experimental import pallas as pl
from jax.experimental.pallas import tpu as pltpu

F32 = jnp.float32
BF16 = jnp.bfloat16
EPS = 1e-6
N_META = 16
LANES = 128
VMEM_LIMIT_BYTES = 56 * 2**20
NEG_BIG = -1e30

S5_GROUP, S5_STATE, S5_GROUPS = 16, 64, 32
S5_WIDTH = S5_GROUPS * S5_GROUP
S5_BUNDLE_GROUPS = LANES // S5_GROUP
S5_BUNDLES = S5_GROUPS // S5_BUNDLE_GROUPS
S5_BSTATE = S5_BUNDLE_GROUPS * S5_STATE
S5_CHUNK = 128
MLA_HEADS, MLA_Q_RANK, MLA_KV_RANK = 8, 512, 256
MLA_NOPE, MLA_ROPE, MLA_V = 128, 64, 128
MLA_QK = MLA_NOPE + MLA_ROPE
ROPE_THETA = 10000.0
ATT_BLOCK = 512
SB_BLOCK = 256
SB_DEAD_TAIL = -104.0
ATT_Q_BLOCK = 512
ATT_HEADS_PER_STEP = 2
SB_HEADS, SB_DIM = 8, 128
SB_W = SB_HEADS * SB_DIM
GDN_HEADS, GDN_DK, GDN_DV, GDN_CONV = 8, 128, 128, 4
GDN_CHUNK = 128
GDN_UNROLL = 8
GDN_HEADS_PER_STEP = 2
GDN_QKV = GDN_HEADS * (2 * GDN_DK + GDN_DV)


def _cparams(*sem):
    return pltpu.CompilerParams(dimension_semantics=sem, vmem_limit_bytes=VMEM_LIMIT_BYTES)


def _row_tile(rows, target):
    best = None
    for t in range(16, min(rows, target) + 1, 16):
        if rows % t == 0:
            best = t
    assert best is not None, rows
    return best


def _rmsnorm(x, g):
    return x * lax.rsqrt(jnp.mean(x * x, axis=-1, keepdims=True) + EPS) * g


def _sigmoid(x):
    return 1.0 / (1.0 + jnp.exp(-x))


def _dot(a, b):
    return jnp.dot(a, b, preferred_element_type=F32)


def _aligned(x, m):
    return x if isinstance(x, int) else pl.multiple_of(x, m)


def _dot_nt(a, b):
    return lax.dot_general(a, b, (((1,), (1,)), ((), ())), preferred_element_type=F32)


def _ffn_kernel(x_ref, g_ref, wg_ref, wu_ref, wd_ref, o_ref, xn_ref):
    j = pl.program_id(1)

    @pl.when(j == 0)
    def _():
        xn_ref[...] = _rmsnorm(x_ref[...], g_ref[...]).astype(BF16)
        o_ref[...] = jnp.zeros_like(o_ref)

    xn = xn_ref[...]
    gate = _dot(xn, wg_ref[...])
    up = _dot(xn, wu_ref[...])
    act = (gate * _sigmoid(gate) * up).astype(BF16)
    o_ref[...] += _dot(act, wd_ref[...])

    @pl.when(j == pl.num_programs(1) - 1)
    def _():
        o_ref[...] = x_ref[...] + 0.5 * o_ref[...]


def _ffn(h, gain, wg, wu, wd, *, tm_target=768, tf=512):
    rows, d = h.shape
    f = wg.shape[1]
    tm = _row_tile(rows, tm_target)
    assert f % tf == 0
    return pl.pallas_call(
        _ffn_kernel,
        grid=(rows // tm, f // tf),
        in_specs=[
            pl.BlockSpec((tm, d), lambda i, j: (i, 0)),
            pl.BlockSpec((1, d), lambda i, j: (0, 0)),
            pl.BlockSpec((d, tf), lambda i, j: (0, j)),
            pl.BlockSpec((d, tf), lambda i, j: (0, j)),
            pl.BlockSpec((tf, d), lambda i, j: (j, 0)),
        ],
        out_specs=pl.BlockSpec((tm, d), lambda i, j: (i, 0)),
        out_shape=jax.ShapeDtypeStruct((rows, d), F32),
        scratch_shapes=[pltpu.VMEM((tm, d), BF16)],
        compiler_params=_cparams("parallel", "arbitrary"),
        name="ffn",
    )(h, gain.reshape(1, d), wg, wu, wd)


def _norm_matmul_kernel(x_ref, g_ref, w_ref, o_ref):
    xn = _rmsnorm(x_ref[...], g_ref[...]).astype(BF16)
    o_ref[...] = _dot(xn, w_ref[...]).astype(o_ref.dtype)


def _norm_matmul(h, gain, w, *, tn, tm_target=768, name="norm_matmul"):
    rows, d = h.shape
    n = w.shape[1]
    tm = _row_tile(rows, tm_target)
    assert n % tn == 0
    return pl.pallas_call(
        _norm_matmul_kernel,
        grid=(n // tn, rows // tm),
        in_specs=[
            pl.BlockSpec((tm, d), lambda j, i: (i, 0)),
            pl.BlockSpec((1, d), lambda j, i: (0, 0)),
            pl.BlockSpec((d, tn), lambda j, i: (0, j), pipeline_mode=pl.Buffered(1)),
        ],
        out_specs=pl.BlockSpec((tm, tn), lambda j, i: (i, j)),
        out_shape=jax.ShapeDtypeStruct((rows, n), F32),
        compiler_params=_cparams("parallel", "parallel"),
        name=name,
    )(h, gain.reshape(1, d), w)


def _out_proj_kernel(h_ref, ya_ref, yb_ref, wa_ref, wb_ref, *rest, glu):
    if glu:
        wglu_ref, o_ref = rest
        y = ya_ref[...]
        gate = _dot(y, wglu_ref[...])
        ya = (y.astype(F32) * _sigmoid(gate)).astype(BF16)
    else:
        (o_ref,) = rest
        ya = ya_ref[...]
    o_ref[...] = h_ref[...] + _dot(ya, wa_ref[...]) + _dot(yb_ref[...], wb_ref[...])


def _out_proj(h, ya, yb, wa, wb, wglu=None, *, tm_target=768, name="out_proj"):
    rows, d = h.shape
    ka, kb = ya.shape[1], yb.shape[1]
    tm = _row_tile(rows, tm_target)
    glu = wglu is not None
    in_specs = [
        pl.BlockSpec((tm, d), lambda i: (i, 0)),
        pl.BlockSpec((tm, ka), lambda i: (i, 0)),
        pl.BlockSpec((tm, kb), lambda i: (i, 0)),
        pl.BlockSpec((ka, d), lambda i: (0, 0)),
        pl.BlockSpec((kb, d), lambda i: (0, 0)),
    ]
    args = [h, ya, yb, wa, wb]
    if glu:
        in_specs.append(pl.BlockSpec((ka, ka), lambda i: (0, 0)))
        args.append(wglu)
    return pl.pallas_call(
        functools.partial(_out_proj_kernel, glu=glu),
        grid=(rows // tm,),
        in_specs=in_specs,
        out_specs=pl.BlockSpec((tm, d), lambda i: (i, 0)),
        out_shape=jax.ShapeDtypeStruct((rows, d), F32),
        compiler_params=_cparams("parallel"),
        name=name,
    )(*args)


def _cmul(ar, ai, br, bi):
    return ar * br - ai * bi, ar * bi + ai * br


def _s5_kernel(u_ref, bb_ref, cm_ref, tneg_ref, tpos_ref, tpos1_ref, d_ref, o_ref, x_s, s_s, *, seq):
    n = S5_BSTATE
    c = S5_CHUNK
    u = u_ref[0]
    x_s[...] = _dot(u.astype(BF16), bb_ref[0])

    def tri(size):
        r = lax.broadcasted_iota(jnp.int32, (size, size), 0)
        col = lax.broadcasted_iota(jnp.int32, (size, size), 1)
        return (r >= col).astype(BF16)

    def chunk(r0, size, ltri, carry):
        cr, ci = carry
        x = x_s[pl.ds(r0, size), :]
        tn = tneg_ref[0, pl.ds(0, size), :]
        tp = tpos_ref[0, pl.ds(0, size), :]
        tq = tpos1_ref[0, pl.ds(0, size), :]
        xr, xi = _cmul(x[:, :n], x[:, n:], tn[:, :n], tn[:, n:])
        acc = _dot(ltri, jnp.concatenate([xr, xi], axis=1).astype(BF16))
        sr, si = _cmul(acc[:, :n], acc[:, n:], tp[:, :n], tp[:, n:])
        qr, qi = _cmul(tq[:, :n], tq[:, n:], cr, ci)
        sr = sr + qr
        si = si + qi
        s_s[pl.ds(r0, size), :] = jnp.concatenate([sr, si], axis=1).astype(BF16)
        return sr[size - 1:size, :], si[size - 1:size, :]

    nfull = seq // c
    tail = seq - nfull * c
    ltri = tri(c)
    zero = jnp.zeros((1, n), F32)
    carry = lax.fori_loop(
        0, nfull, lambda i, cy: chunk(pl.multiple_of(i * c, c), c, ltri, cy), (zero, zero))
    if tail:
        chunk(nfull * c, tail, tri(tail), carry)
    y = _dot(s_s[...], cm_ref[0]) + d_ref[0] * u
    o_ref[0] = jax.nn.gelu(y, approximate=True).astype(o_ref.dtype)


def _s5(p3d, tabs):
    bsz, seq, _ = p3d.shape
    n2 = 2 * S5_BSTATE
    tab_spec = pl.BlockSpec((1, S5_CHUNK, n2), lambda b, k: (k, 0, 0))
    return pl.pallas_call(
        functools.partial(_s5_kernel, seq=seq),
        grid=(bsz, S5_BUNDLES),
        in_specs=[
            pl.BlockSpec((1, seq, LANES), lambda b, k: (b, 0, k)),
            pl.BlockSpec((1, LANES, n2), lambda b, k: (k, 0, 0)),
            pl.BlockSpec((1, n2, LANES), lambda b, k: (k, 0, 0)),
            tab_spec, tab_spec, tab_spec,
            pl.BlockSpec((1, 1, LANES), lambda b, k: (k, 0, 0)),
        ],
        out_specs=pl.BlockSpec((1, seq, LANES), lambda b, k: (b, 0, k)),
        out_shape=jax.ShapeDtypeStruct((bsz, seq, S5_WIDTH), BF16),
        scratch_shapes=[pltpu.VMEM((seq, n2), F32), pltpu.VMEM((seq, n2), BF16)],
        compiler_params=_cparams("parallel", "parallel"),
        name="s5",
    )(p3d, tabs["bbar"], tabs["cmat"], tabs["tneg"], tabs["tpos"], tabs["tpos1"], tabs["dskip"])


def _s5_tables(log_dt, a_re, a_im, b_re, b_im, c_re, c_im, d_skip):
    g, p, c = S5_GROUPS, S5_STATE, S5_GROUP
    nb, gb = S5_BUNDLES, S5_BUNDLE_GROUPS
    dt = jnp.exp(log_dt.astype(F32))[:, None]
    ar, ai = a_re.astype(F32), a_im.astype(F32)
    mag = jnp.exp(dt * ar)
    lam_r, lam_i = mag * jnp.cos(dt * ai), mag * jnp.sin(dt * ai)
    den = ar * ar + ai * ai
    coef_r = ((lam_r - 1.0) * ar + lam_i * ai) / den
    coef_i = (lam_i * ar - (lam_r - 1.0) * ai) / den
    br, bi = b_re.astype(F32), b_im.astype(F32)
    bbar_r = coef_r[..., None] * br - coef_i[..., None] * bi
    bbar_i = coef_r[..., None] * bi + coef_i[..., None] * br
    eye = jnp.eye(gb, dtype=F32)

    def block_in(m):
        m = m.reshape(nb, gb, p, c)
        return jnp.einsum("kgpc,gh->kgchp", m, eye).reshape(nb, gb * c, gb * p)

    def block_out(m):
        m = m.reshape(nb, gb, c, p)
        return jnp.einsum("kgcp,gh->kgphc", m, eye).reshape(nb, gb * p, gb * c)

    bbar = jnp.concatenate([block_in(bbar_r), block_in(bbar_i)], axis=2).astype(BF16)
    cmat = jnp.concatenate([block_out(c_re.astype(F32)), -block_out(c_im.astype(F32))], axis=1).astype(BF16)

    def power_table(offset, sign):
        e = sign * (jnp.arange(S5_CHUNK, dtype=F32) + offset)[:, None, None]
        m = jnp.exp(e * (dt * ar)[None])
        re = (m * jnp.cos(e * (dt * ai)[None])).reshape(S5_CHUNK, nb, gb * p)
        im = (m * jnp.sin(e * (dt * ai)[None])).reshape(S5_CHUNK, nb, gb * p)
        return jnp.transpose(jnp.concatenate([re, im], axis=2), (1, 0, 2))

    return {
        "bbar": bbar, "cmat": cmat,
        "tneg": power_table(0.0, -1.0), "tpos": power_table(0.0, 1.0), "tpos1": power_table(1.0, 1.0),
        "dskip": d_skip.astype(F32).reshape(nb, 1, gb * c),
    }


def _query_blocks(seq, qb, kb):
    assert qb % kb == 0 and seq >= qb and seq % qb <= kb
    nloop = seq // qb - 1
    last_r0 = nloop * qb
    last_rows = seq - last_r0
    diag = [(d * kb, kb) for d in range(qb // kb)]
    last_diag = diag + ([(qb, last_rows - qb)] if last_rows > qb else [])
    return nloop, last_r0, last_rows, diag, last_diag


def _softmax_steps(ss, vs, carries):
    ms, ls, accs = zip(*carries)
    m_news = [jnp.maximum(m, jnp.max(s, axis=-1, keepdims=True)) for m, s in zip(ms, ss)]
    ps = [jnp.exp(s - mn) for s, mn in zip(ss, m_news)]
    alphas = [jnp.exp(m - mn) for m, mn in zip(ms, m_news)]
    ls = [a * l + jnp.sum(p, axis=-1, keepdims=True) for a, l, p in zip(alphas, ls, ps)]
    pvs = [_dot(p.astype(BF16), v) for p, v in zip(ps, vs)]
    accs = [a * acc + pv for a, acc, pv in zip(alphas, accs, pvs)]
    return tuple(zip(m_news, ls, accs))


def _mla_kernel(cq_ref, ckv_ref, kr_ref, gcq_ref, gckv_ref, wq_ref, wkv_ref, gq_ref, gkn_ref, gkr_ref,
                cs_ref, o_ref, cqn_s, ckvn_s, krope_s, krss_s, q_s, k_s, v_s, *, seq):
    qb, kb = ATT_Q_BLOCK, ATT_BLOCK
    heads = range(q_s.shape[0])
    lane = lax.broadcasted_iota(jnp.int32, (seq, LANES), 1)
    first_half = lane < MLA_ROPE
    cs = cs_ref[...]

    def rope_part(t):
        t = t * cs
        t = t + pltpu.roll(t, MLA_ROPE, axis=1)
        return jnp.where(first_half, t, 0.0)

    @pl.when(pl.program_id(1) == 0)
    def _():
        cqn_s[...] = _rmsnorm(cq_ref[0], gcq_ref[...]).astype(BF16)
        ckvn_s[...] = _rmsnorm(ckv_ref[0], gckv_ref[...]).astype(BF16)
        kr = kr_ref[0]
        krope_s[...] = rope_part(kr * gkr_ref[...])
        ss = jnp.sum(jnp.where(first_half, kr * kr, 0.0), axis=-1, keepdims=True)
        krss_s[...] = jnp.broadcast_to(ss, (seq, LANES))

    gq = gq_ref[...]
    xqs = [_dot(cqn_s[...], wq_ref[h]) for h in heads]
    sqs = [lax.rsqrt(jnp.sum(x[:, :LANES] * x[:, :LANES] + jnp.where(first_half, x[:, LANES:] * x[:, LANES:], 0.0),
                             axis=-1, keepdims=True) * (1.0 / MLA_QK) + EPS) * (MLA_QK ** -0.5) for x in xqs]
    for h, x, sq in zip(heads, xqs, sqs):
        q_s[h] = jnp.concatenate(
            [x[:, :LANES] * (sq * gq[:, :LANES]), rope_part(x[:, LANES:] * (sq * gq[:, LANES:]))],
            axis=1).astype(BF16)
    xkvs = [_dot(ckvn_s[...], wkv_ref[h]) for h in heads]
    sks = [lax.rsqrt((jnp.sum(x[:, :LANES] * x[:, :LANES], axis=-1, keepdims=True) + krss_s[...])
                     * (1.0 / MLA_QK) + EPS) for x in xkvs]
    for h, x, sk in zip(heads, xkvs, sks):
        k_s[h] = jnp.concatenate([x[:, :LANES] * (sk * gkn_ref[...]), krope_s[...] * sk], axis=1).astype(BF16)
        v_s[h] = x[:, LANES:].astype(BF16)

    def q_block(r0, rows, nprev, diag):
        qs = [q_s[h, pl.ds(r0, rows), :] for h in heads]

        def scores(c0, size):
            return ([_dot_nt(q, k_s[h, pl.ds(c0, size), :]) for h, q in zip(heads, qs)],
                    [v_s[h, pl.ds(c0, size), :] for h in heads])

        def body(j, carry):
            return _softmax_steps(*scores(pl.multiple_of(j * kb, kb), kb), carry)

        init = (jnp.full((rows, 1), NEG_BIG, F32), jnp.zeros((rows, 1), F32), jnp.zeros((rows, MLA_V), F32))
        carry = lax.fori_loop(0, nprev, body, tuple(init for _ in heads))
        for off, size in diag:
            ss, vs = scores(r0 + off, size)
            r = lax.broadcasted_iota(jnp.int32, (rows, size), 0)
            c = lax.broadcasted_iota(jnp.int32, (rows, size), 1)
            carry = _softmax_steps([jnp.where(c + off <= r, s, NEG_BIG) for s in ss], vs, carry)
        for h, (_, l, acc) in zip(heads, carry):
            o_ref[0, pl.ds(r0, rows), h * MLA_V:(h + 1) * MLA_V] = (acc / l).astype(o_ref.dtype)

    nloop, last_r0, last_rows, diag, last_diag = _query_blocks(seq, qb, kb)

    def outer(i, _):
        q_block(pl.multiple_of(i * qb, qb), qb, i * (qb // kb), diag)
        return 0

    lax.fori_loop(0, nloop, outer, 0)
    q_block(last_r0, last_rows, last_r0 // kb, last_diag)


def _mla(p3d, g_cq, g_ckv, wq, wkv, gq, gkn, gkr, cs):
    bsz, seq, _ = p3d.shape
    hp = ATT_HEADS_PER_STEP
    full = lambda shape: pl.BlockSpec(shape, lambda b, g: (0,) * len(shape))
    return pl.pallas_call(
        functools.partial(_mla_kernel, seq=seq),
        grid=(bsz, MLA_HEADS // hp),
        in_specs=[
            pl.BlockSpec((1, seq, MLA_Q_RANK), lambda b, g: (b, 0, S5_WIDTH // MLA_Q_RANK)),
            pl.BlockSpec((1, seq, MLA_KV_RANK), lambda b, g: (b, 0, (S5_WIDTH + MLA_Q_RANK) // MLA_KV_RANK)),
            pl.BlockSpec((1, seq, LANES), lambda b, g: (b, 0, (S5_WIDTH + MLA_Q_RANK + MLA_KV_RANK) // LANES)),
            full((1, MLA_Q_RANK)), full((1, MLA_KV_RANK)),
            pl.BlockSpec((hp, MLA_Q_RANK, 2 * LANES), lambda b, g: (g, 0, 0)),
            pl.BlockSpec((hp, MLA_KV_RANK, 2 * LANES), lambda b, g: (g, 0, 0)),
            full((1, 2 * LANES)), full((1, LANES)), full((1, LANES)),
            full((seq, LANES)),
        ],
        out_specs=pl.BlockSpec((1, seq, hp * MLA_V), lambda b, g: (b, 0, g)),
        out_shape=jax.ShapeDtypeStruct((bsz, seq, MLA_HEADS * MLA_V), BF16),
        scratch_shapes=[pltpu.VMEM((seq, MLA_Q_RANK), BF16), pltpu.VMEM((seq, MLA_KV_RANK), BF16),
                        pltpu.VMEM((seq, LANES), F32), pltpu.VMEM((seq, LANES), F32),
                        pltpu.VMEM((hp, seq, 2 * LANES), BF16), pltpu.VMEM((hp, seq, 2 * LANES), BF16),
                        pltpu.VMEM((hp, seq, MLA_V), BF16)],
        compiler_params=_cparams("parallel", "arbitrary"),
        name="mla",
    )(p3d, p3d, p3d, g_cq, g_ckv, wq, wkv, gq, gkn, gkr, cs)


def _rot_cols(w):
    half = MLA_ROPE // 2
    return jnp.concatenate([-w[..., half:], w[..., :half]], axis=-1)


def _swap_halves(g):
    half = MLA_ROPE // 2
    return jnp.concatenate([g[..., half:], g[..., :half]], axis=-1)


def _mla_params(w_uq, w_ukv, g_q, g_k, seq):
    wq = w_uq.reshape(MLA_Q_RANK, MLA_HEADS, MLA_QK)
    wq = jnp.concatenate([wq, _rot_cols(wq[..., MLA_NOPE:])], axis=-1)
    wq = jnp.transpose(wq, (1, 0, 2)).astype(BF16)
    wkv = jnp.transpose(w_ukv.reshape(MLA_KV_RANK, MLA_HEADS, MLA_NOPE + MLA_V), (1, 0, 2)).astype(BF16)
    gq = jnp.concatenate([g_q, _swap_halves(g_q[MLA_NOPE:])]).reshape(1, 2 * LANES).astype(F32)
    gkn = g_k[:MLA_NOPE].reshape(1, LANES).astype(F32)
    gkr = jnp.concatenate([g_k[MLA_NOPE:], _swap_halves(g_k[MLA_NOPE:])]).reshape(1, LANES).astype(F32)
    half = MLA_ROPE // 2
    inv = ROPE_THETA ** (-jnp.arange(half, dtype=F32) / half)
    ang = jnp.arange(seq, dtype=F32)[:, None] * inv[None, :]
    cs = jnp.concatenate([jnp.cos(ang), jnp.cos(ang), jnp.sin(ang), jnp.sin(ang)], axis=1)
    return wq, wkv, gq, gkn, gkr, cs


def _split2(x):
    hi = x.astype(BF16)
    return hi, (x - hi.astype(F32)).astype(BF16)


def _split3(x):
    hi = x.astype(BF16)
    r = x - hi.astype(F32)
    mid = r.astype(BF16)
    return hi, mid, (r - mid.astype(F32)).astype(BF16)


def _sb_kernel(q_ref, k_ref, v_ref, o_ref, q_s, k_s, v_s, *, seq):
    qb, blk = ATT_Q_BLOCK, SB_BLOCK
    heads = range(q_s.shape[0])
    for h in heads:
        cols = slice(h * SB_DIM, (h + 1) * SB_DIM)
        q_s[h] = (q_ref[0, :, cols] * (SB_DIM ** -0.5)).astype(BF16)
        k_s[h] = k_ref[0, :, cols].astype(BF16)
        v_s[h] = v_ref[0, :, cols].astype(BF16)

    def later_keys(size):
        r = lax.broadcasted_iota(jnp.int32, (size, size), 0)
        c = lax.broadcasted_iota(jnp.int32, (size, size), 1)
        return (r > c).astype(BF16)

    def tiles(qs, c0, size, u_mat, strict, carries):
        tails, accs = zip(*carries)
        zs = [_dot_nt(q, k_s[h, pl.ds(c0, size), :]) for h, q in zip(heads, qs)]
        log_keeps = [-(jnp.maximum(z, 0.0) + jnp.log(1.0 + jnp.exp(-jnp.abs(z)))) for z in zs]
        if strict is not None:
            log_keeps = [jnp.where(strict, lk, 0.0) for lk in log_keeps]
        splits = [_split2(lk) for lk in log_keeps]
        inners = [_dot(hi, u_mat) + _dot(lo, u_mat) for hi, lo in splits]
        ws = [jnp.exp(z + lk + inner + tail) for z, lk, inner, tail in zip(zs, log_keeps, inners, tails)]
        if strict is not None:
            ws = [jnp.where(strict, w, 0.0) for w in ws]
        accs = [acc + _dot(w.astype(BF16), v_s[h, pl.ds(c0, size), :]) for h, w, acc in zip(heads, ws, accs)]
        tails = [tail + jnp.sum(lk, axis=-1, keepdims=True) for tail, lk in zip(tails, log_keeps)]
        return tuple(zip(tails, accs))

    u_blk = later_keys(blk)

    def q_block(r0, rows, nprev, diag):
        qs = [q_s[h, pl.ds(r0, rows), :] for h in heads]
        carry = tuple((jnp.zeros((rows, 1), F32), jnp.zeros((rows, SB_DIM), F32)) for _ in heads)
        for off, size, u_mat in reversed(diag):
            r = lax.broadcasted_iota(jnp.int32, (rows, size), 0)
            c = lax.broadcasted_iota(jnp.int32, (rows, size), 1)
            carry = tiles(qs, r0 + off, size, u_mat, c + off < r, carry)

        def live(cy):
            tail_max = functools.reduce(jnp.maximum, [tail for tail, _ in cy])
            return (jnp.max(tail_max) > SB_DEAD_TAIL).astype(jnp.int32)

        def body(state):
            jj, _, cy = state
            c0 = pl.multiple_of((nprev - 1 - jj) * blk, blk)
            cy = tiles(qs, c0, blk, u_blk, None, cy)
            return jj + 1, live(cy), cy

        _, _, carry = lax.while_loop(lambda s: (s[0] < nprev) & (s[1] > 0), body,
                                     (jnp.int32(0), live(carry), carry))
        for h, (_, acc) in zip(heads, carry):
            o_ref[0, pl.ds(r0, rows), h * SB_DIM:(h + 1) * SB_DIM] = acc.astype(o_ref.dtype)

    nloop, last_r0, last_rows, diag, last_diag = _query_blocks(seq, qb, blk)
    u_of = lambda size: u_blk if size == blk else later_keys(size)

    def outer(i, _):
        q_block(pl.multiple_of(i * qb, qb), qb, i * (qb // blk), [(o, s, u_of(s)) for o, s in diag])
        return 0

    lax.fori_loop(0, nloop, outer, 0)
    q_block(last_r0, last_rows, last_r0 // blk, [(o, s, u_of(s)) for o, s in last_diag])


def _stick_breaking(p3d):
    bsz, seq, _ = p3d.shape
    hp = ATT_HEADS_PER_STEP
    groups = SB_HEADS // hp
    spec = lambda off: pl.BlockSpec((1, seq, hp * SB_DIM), lambda b, g: (b, 0, off + g))
    return pl.pallas_call(
        functools.partial(_sb_kernel, seq=seq),
        grid=(bsz, groups),
        in_specs=[spec(0), spec(groups), spec(2 * groups)],
        out_specs=pl.BlockSpec((1, seq, hp * SB_DIM), lambda b, g: (b, 0, g)),
        out_shape=jax.ShapeDtypeStruct((bsz, seq, SB_W), BF16),
        scratch_shapes=[pltpu.VMEM((hp, seq, SB_DIM), BF16)] * 3,
        compiler_params=_cparams("parallel", "parallel"),
        name="stick_breaking",
    )(p3d, p3d, p3d)


def _gdn_kernel(q_ref, k_ref, v_ref, z_ref, ab_ref, cwq_ref, cwk_ref, cwv_ref, gate_ref, gout_ref, o_ref,
                pad_s, q_s, k_s, v_s, g_s, b_s, o_s, u_s, w_s, a_s, qd_s, kdt_s, gl_s, *, seq):
    c = GDN_CHUNK
    hp = q_s.shape[0]
    nchunks = pl.cdiv(seq, c)
    padded = nchunks * c
    pad_s[pl.ds(0, 8), :] = jnp.zeros((8, LANES), F32)

    def conv_silu(x_ref, w_ref, cols):
        pad_s[pl.ds(8, seq), :] = x_ref[0, :, cols]
        w = w_ref[:, cols]
        y = w[GDN_CONV - 1:GDN_CONV, :] * pad_s[pl.ds(8, seq), :]
        for j in range(GDN_CONV - 1):
            y = y + w[j:j + 1, :] * pad_s[pl.ds(8 - (GDN_CONV - 1) + j, seq), :]
        return y * _sigmoid(y)

    def l2norm(x):
        return x * lax.rsqrt(jnp.sum(x * x, axis=-1, keepdims=True) + EPS)

    ab = ab_ref[0]
    lane = lax.broadcasted_iota(jnp.int32, (seq, LANES), 1)
    gate = gate_ref[...]
    t = ab + gate[1:2, :]
    g_all = -jnp.exp(gate[0:1, :]) * (jnp.maximum(t, 0.0) + jnp.log(1.0 + jnp.exp(-jnp.abs(t))))
    beta_all = _sigmoid(ab)
    zeros_tail = jnp.zeros((padded - seq, LANES), F32)

    def prepare(hd):
        cols = slice(hd * LANES, (hd + 1) * LANES)
        head = pl.program_id(1) * hp + hd
        if padded > seq:
            for ref in (q_s, k_s, v_s, g_s, b_s):
                ref[hd, pl.ds(seq, padded - seq), :] = zeros_tail
        q_s[hd, pl.ds(0, seq), :] = l2norm(conv_silu(q_ref, cwq_ref, cols)) * (GDN_DK ** -0.5)
        k_s[hd, pl.ds(0, seq), :] = l2norm(conv_silu(k_ref, cwk_ref, cols))
        v_s[hd, pl.ds(0, seq), :] = conv_silu(v_ref, cwv_ref, cols)
        g_col = jnp.sum(jnp.where(lane == head, g_all, 0.0), axis=-1, keepdims=True)
        b_col = jnp.sum(jnp.where(lane == head + GDN_HEADS, beta_all, 0.0), axis=-1, keepdims=True)
        g_s[hd, pl.ds(0, seq), :] = jnp.broadcast_to(g_col, (seq, LANES))
        b_s[hd, pl.ds(0, seq), :] = jnp.broadcast_to(b_col, (seq, LANES))

    r = lax.broadcasted_iota(jnp.int32, (c, c), 0)
    col = lax.broadcasted_iota(jnp.int32, (c, c), 1)
    lower = r >= col
    strict = r > col
    incl = lower.astype(BF16)
    eye = (r == col).astype(F32)
    level_masks = [((r >> l) == (col >> l)) & ((r >> (l - 1)) != (col >> (l - 1))) & strict
                   for l in range(1, int(math.log2(c)) + 1)]

    def local(hd, first, count):
        ids = [first + t for t in range(count)]
        rows = [pl.ds(_aligned(i * c, c), c) for i in ids]
        ks = [k_s[hd, r, :] for r in rows]
        betas = [b_s[hd, r, :] for r in rows]
        gbs = [g_s[hd, r, :] for r in rows]
        sums = [_dot(incl, jnp.concatenate(_split3(gb) + _split3(jnp.where(strict, gb, 0.0)), axis=1))
                for gb in gbs]
        cums = [s[:, 0:c] + s[:, c:2 * c] + s[:, 2 * c:3 * c] for s in sums]
        decays = [jnp.exp(jnp.where(lower, s[:, 3 * c:4 * c] + s[:, 4 * c:5 * c] + s[:, 5 * c:6 * c], NEG_BIG))
                  for s in sums]
        kbs = [k * beta for k, beta in zip(ks, betas)]
        kbfs = [k.astype(BF16) for k in ks]
        ms = [jnp.where(strict, _dot_nt(kb.astype(BF16), kbf) * decay, 0.0)
              for kb, kbf, decay in zip(kbs, kbfs, decays)]
        invs = [eye.astype(BF16)] * count
        for mask in level_masks:
            ts = [_dot(jnp.where(mask, m, 0.0).astype(BF16), inv).astype(BF16) for m, inv in zip(ms, invs)]
            invs = [(inv.astype(F32) - _dot(inv, t)).astype(BF16) for inv, t in zip(invs, ts)]
        e_cums = [jnp.exp(cum) for cum in cums]
        for r, inv, kb, e_cum in zip(rows, invs, kbs, e_cums):
            w_s[hd, r, :] = _dot(inv, (kb * e_cum).astype(BF16)).astype(BF16)
        for r, inv, beta in zip(rows, invs, betas):
            u_s[hd, r, :] = _dot(inv, (v_s[hd, r, :] * beta).astype(BF16))
        qs = [q_s[hd, r, :] for r in rows]
        for r, q, kbf, decay in zip(rows, qs, kbfs, decays):
            a_s[hd, r, :] = jnp.where(lower, _dot_nt(q.astype(BF16), kbf) * decay, 0.0).astype(BF16)
        for i, r, q, k, cum, e_cum in zip(ids, rows, qs, ks, cums, e_cums):
            qd_s[hd, r, :] = (q * e_cum).astype(BF16)
            last = cum[c - 1:c, :]
            kdt_s[hd, r, :] = (k * jnp.exp(last - cum)).T.astype(BF16)
            gl_s[hd, pl.ds(_aligned(i * 8, 8), 8), :] = jnp.broadcast_to(jnp.exp(last), (8, LANES))

    def recur(i, states):
        rows = pl.ds(pl.multiple_of(i * c, c), c)
        heads = range(hp)
        sbs = [s.astype(BF16) for s in states]
        v_news = [(u_s[hd, rows, :] - _dot(w_s[hd, rows, :], sb)).astype(BF16) for hd, sb in zip(heads, sbs)]
        for hd, sb, v_new in zip(heads, sbs, v_news):
            o_s[hd, rows, :] = _dot(qd_s[hd, rows, :], sb) + _dot(a_s[hd, rows, :], v_new)
        return tuple(s * gl_s[hd, pl.ds(pl.multiple_of(i * 8, 8), 1), :] + _dot(kdt_s[hd, rows, :], v_new)
                     for hd, s, v_new in zip(heads, states, v_news))

    ngroups = max(nchunks // GDN_UNROLL, 1)
    for hd in range(hp):
        prepare(hd)
        for gi in range(ngroups):
            first = gi * GDN_UNROLL
            local(hd, first, GDN_UNROLL if gi < ngroups - 1 else nchunks - first)
    lax.fori_loop(0, nchunks, recur, tuple(jnp.zeros((GDN_DK, GDN_DV), F32) for _ in range(hp)))
    for hd in range(hp):
        cols = slice(hd * LANES, (hd + 1) * LANES)
        z = z_ref[0, :, cols]
        o = _rmsnorm(o_s[hd, pl.ds(0, seq), :], gout_ref[...]) * (z * _sigmoid(z))
        o_ref[0, :, cols] = o.astype(o_ref.dtype)


def _gated_deltanet(p3d, conv_w, gate, g_out):
    bsz, seq, _ = p3d.shape
    hp = GDN_HEADS_PER_STEP
    groups = GDN_HEADS // hp
    off = 3 * SB_HEADS // hp
    spec = lambda o: pl.BlockSpec((1, seq, hp * LANES), lambda b, g: (b, 0, o + g))
    cspec = lambda o: pl.BlockSpec((GDN_CONV, hp * LANES), lambda b, g: (0, o + g))
    padded = pl.cdiv(seq, GDN_CHUNK) * GDN_CHUNK
    return pl.pallas_call(
        functools.partial(_gdn_kernel, seq=seq),
        grid=(bsz, groups),
        in_specs=[
            spec(off), spec(off + groups), spec(off + 2 * groups), spec(off + 3 * groups),
            pl.BlockSpec((1, seq, LANES), lambda b, g: (b, 0, (3 * SB_HEADS + 4 * GDN_HEADS))),
            cspec(0), cspec(groups), cspec(2 * groups),
            pl.BlockSpec((2, LANES), lambda b, g: (0, 0)),
            pl.BlockSpec((1, LANES), lambda b, g: (0, 0)),
        ],
        out_specs=pl.BlockSpec((1, seq, hp * LANES), lambda b, g: (b, 0, g)),
        out_shape=jax.ShapeDtypeStruct((bsz, seq, GDN_HEADS * GDN_DV), BF16),
        scratch_shapes=([pltpu.VMEM((seq + 8, LANES), F32)] + [pltpu.VMEM((hp, padded, LANES), F32)] * 7
                        + [pltpu.VMEM((hp, padded, LANES), BF16)] * 4
                        + [pltpu.VMEM((hp, 8 * padded // GDN_CHUNK, LANES), F32)]),
        compiler_params=_cparams("parallel", "parallel"),
        name="gated_deltanet",
    )(p3d, p3d, p3d, p3d, p3d, conv_w, conv_w, conv_w, gate, g_out)


def _even_mixer(h, bsz, seq, gain, w_in, s5_params, w_glu, g_cq, g_ckv, w_uq, w_ukv, g_q, g_k, w_out):
    d = h.shape[1]
    rope0 = S5_WIDTH + MLA_Q_RANK + MLA_KV_RANK
    w_ext = jnp.concatenate([w_in, _rot_cols(w_in[:, rope0:])], axis=1).astype(BF16)
    p = _norm_matmul(h, gain, w_ext, tn=w_ext.shape[1], name="even_in_proj")
    p3d = p.reshape(bsz, seq, -1)
    y_a = _s5(p3d, _s5_tables(*s5_params))
    wq, wkv, gq, gkn, gkr, cs = _mla_params(w_uq, w_ukv, g_q, g_k, seq)
    y_b = _mla(p3d, g_cq.reshape(1, -1), g_ckv.reshape(1, -1), wq, wkv, gq, gkn, gkr, cs)
    return _out_proj(h, y_a.reshape(-1, S5_WIDTH), y_b.reshape(-1, MLA_HEADS * MLA_V),
                     w_out[:S5_WIDTH].astype(BF16), w_out[S5_WIDTH:].astype(BF16), w_glu.astype(BF16),
                     name="even_out_proj")


def _odd_mixer(h, bsz, seq, gain, w_in, conv_w, a_log, dt_bias, g_out, w_out):
    d = h.shape[1]
    o1 = 3 * SB_W + GDN_QKV
    ab = w_in[:, o1:o1 + 2 * GDN_HEADS]
    w_ext = jnp.concatenate(
        [w_in[:, :o1], w_in[:, o1 + 2 * GDN_HEADS:], ab, jnp.zeros((d, LANES - 2 * GDN_HEADS), w_in.dtype)],
        axis=1).astype(BF16)
    p = _norm_matmul(h, gain, w_ext, tn=w_ext.shape[1] // 3, name="odd_in_proj")
    p3d = p.reshape(bsz, seq, -1)
    y_c = _stick_breaking(p3d)
    pad = jnp.zeros((LANES - GDN_HEADS,), F32)
    gate = jnp.stack([jnp.concatenate([a_log.astype(F32), pad]), jnp.concatenate([dt_bias.astype(F32), pad])])
    y_d = _gated_deltanet(p3d, conv_w.astype(F32), gate, g_out.reshape(1, -1).astype(F32))
    return _out_proj(h, y_c.reshape(-1, SB_W), y_d.reshape(-1, GDN_HEADS * GDN_DV),
                     w_out[:SB_W].astype(BF16), w_out[SB_W:].astype(BF16), name="odd_out_proj")


def kernel(x, meta_tokens, norm_ffn1, w1_gate, w1_up, w1_down, norm_mix, norm_ffn2, w2_gate, w2_up, w2_down, ev_w_in, s5_log_dt, s5_a_re, s5_a_im, s5_b_re, s5_b_im, s5_c_re, s5_c_im, s5_d, s5_w_glu, mla_g_cq, mla_g_ckv, mla_w_uq, mla_w_ukv, mla_g_q, mla_g_k, ev_w_out, od_w_in, gdn_conv, gdn_a_log, gdn_dt_bias, gdn_g_out, od_w_out):
    bsz, _, d = x.shape
    depth = norm_ffn1.shape[0]
    meta = jnp.broadcast_to(meta_tokens[None].astype(x.dtype), (bsz, N_META, d))
    h3 = jnp.concatenate([meta, x], axis=1)
    seq = h3.shape[1]
    h = h3.reshape(bsz * seq, d)
    for l in range(depth):
        i = l // 2
        h = _ffn(h, norm_ffn1[l], w1_gate[l].astype(BF16), w1_up[l].astype(BF16), w1_down[l].astype(BF16))
        if l % 2 == 0:
            s5_params = (s5_log_dt[i], s5_a_re[i], s5_a_im[i], s5_b_re[i], s5_b_im[i], s5_c_re[i],
                         s5_c_im[i], s5_d[i])
            h = _even_mixer(h, bsz, seq, norm_mix[l], ev_w_in[i], s5_params, s5_w_glu[i], mla_g_cq[i],
                            mla_g_ckv[i], mla_w_uq[i], mla_w_ukv[i], mla_g_q[i], mla_g_k[i], ev_w_out[i])
        else:
            h = _odd_mixer(h, bsz, seq, norm_mix[l], od_w_in[i], gdn_conv[i], gdn_a_log[i], gdn_dt_bias[i],
                           gdn_g_out[i], od_w_out[i])
        h = _ffn(h, norm_ffn2[l], w2_gate[l].astype(BF16), w2_up[l].astype(BF16), w2_down[l].astype(BF16))
    return h.reshape(bsz, seq, d)[:, N_META:]
```

```python
import functools
import math

import jax
import jax.numpy as jnp
from jax import lax
from jax.experimental import pallas as pl
from jax.experimental.pallas import tpu as pltpu

F32 = jnp.float32
BF16 = jnp.bfloat16
EPS = 1e-6
N_META = 16
LANES = 128
VMEM_LIMIT_BYTES = 56 * 2**20
NEG_BIG = -1e30

S5_GROUP, S5_STATE, S5_GROUPS = 16, 64, 32
S5_WIDTH = S5_GROUPS * S5_GROUP
S5_BUNDLE_GROUPS = LANES // S5_GROUP
S5_BUNDLES = S5_GROUPS // S5_BUNDLE_GROUPS
S5_BSTATE = S5_BUNDLE_GROUPS * S5_STATE
S5_CHUNK = 128
S5_SUB = 16
MLA_HEADS, MLA_Q_RANK, MLA_KV_RANK = 8, 512, 256
MLA_NOPE, MLA_ROPE, MLA_V = 128, 64, 128
MLA_QK = MLA_NOPE + MLA_ROPE
ROPE_THETA = 10000.0
ATT_BLOCK = 512
SB_BLOCK = 256
SB_DEAD_TAIL = -104.0
ATT_Q_BLOCK = 512
ATT_HEADS_PER_STEP = 2
SB_HEADS, SB_DIM = 8, 128
SB_W = SB_HEADS * SB_DIM
GDN_HEADS, GDN_DK, GDN_DV, GDN_CONV = 8, 128, 128, 4
GDN_CHUNK = 128
GDN_UNROLL = 8
GDN_HEADS_PER_STEP = 2
GDN_QKV = GDN_HEADS * (2 * GDN_DK + GDN_DV)


def _cparams(*sem):
    return pltpu.CompilerParams(dimension_semantics=sem, vmem_limit_bytes=VMEM_LIMIT_BYTES)


def _row_tile(rows, target):
    best = None
    for t in range(16, min(rows, target) + 1, 16):
        if rows % t == 0:
            best = t
    assert best is not None, rows
    return best


def _rmsnorm(x, g):
    return x * lax.rsqrt(jnp.mean(x * x, axis=-1, keepdims=True) + EPS) * g


def _sigmoid(x):
    return 1.0 / (1.0 + jnp.exp(-x))


def _dot(a, b):
    return jnp.dot(a, b, preferred_element_type=F32)


def _aligned(x, m):
    return x if isinstance(x, int) else pl.multiple_of(x, m)


def _dot_nt(a, b):
    return lax.dot_general(a, b, (((1,), (1,)), ((), ())), preferred_element_type=F32)


def _ffn_kernel(x_ref, g_ref, wg_ref, wu_ref, wd_ref, o_ref, xn_ref):
    j = pl.program_id(1)

    @pl.when(j == 0)
    def _():
        xn_ref[...] = _rmsnorm(x_ref[...], g_ref[...]).astype(BF16)
        o_ref[...] = jnp.zeros_like(o_ref)

    xn = xn_ref[...]
    gate = _dot(xn, wg_ref[...])
    up = _dot(xn, wu_ref[...])
    act = (gate * _sigmoid(gate) * up).astype(BF16)
    o_ref[...] += _dot(act, wd_ref[...])

    @pl.when(j == pl.num_programs(1) - 1)
    def _():
        o_ref[...] = x_ref[...] + 0.5 * o_ref[...]


def _ffn(h, gain, wg, wu, wd, *, tm_target=768, tf=512):
    rows, d = h.shape
    f = wg.shape[1]
    tm = _row_tile(rows, tm_target)
    assert f % tf == 0
    return pl.pallas_call(
        _ffn_kernel,
        grid=(rows // tm, f // tf),
        in_specs=[
            pl.BlockSpec((tm, d), lambda i, j: (i, 0)),
            pl.BlockSpec((1, d), lambda i, j: (0, 0)),
            pl.BlockSpec((d, tf), lambda i, j: (0, j)),
            pl.BlockSpec((d, tf), lambda i, j: (0, j)),
            pl.BlockSpec((tf, d), lambda i, j: (j, 0)),
        ],
        out_specs=pl.BlockSpec((tm, d), lambda i, j: (i, 0)),
        out_shape=jax.ShapeDtypeStruct((rows, d), F32),
        scratch_shapes=[pltpu.VMEM((tm, d), BF16)],
        compiler_params=_cparams("parallel", "arbitrary"),
        name="ffn",
    )(h, gain.reshape(1, d), wg, wu, wd)


def _norm_matmul_kernel(x_ref, g_ref, w_ref, o_ref):
    xn = _rmsnorm(x_ref[...], g_ref[...]).astype(BF16)
    o_ref[...] = _dot(xn, w_ref[...]).astype(o_ref.dtype)


def _norm_matmul(h, gain, w, *, tn, tm_target=768, name="norm_matmul"):
    rows, d = h.shape
    n = w.shape[1]
    tm = _row_tile(rows, tm_target)
    assert n % tn == 0
    return pl.pallas_call(
        _norm_matmul_kernel,
        grid=(n // tn, rows // tm),
        in_specs=[
            pl.BlockSpec((tm, d), lambda j, i: (i, 0)),
            pl.BlockSpec((1, d), lambda j, i: (0, 0)),
            pl.BlockSpec((d, tn), lambda j, i: (0, j), pipeline_mode=pl.Buffered(1)),
        ],
        out_specs=pl.BlockSpec((tm, tn), lambda j, i: (i, j)),
        out_shape=jax.ShapeDtypeStruct((rows, n), F32),
        compiler_params=_cparams("parallel", "parallel"),
        name=name,
    )(h, gain.reshape(1, d), w)


def _out_proj_kernel(h_ref, ya_ref, yb_ref, wa_ref, wb_ref, *rest, glu):
    if glu:
        wglu_ref, o_ref = rest
        y = ya_ref[...]
        gate = _dot(y, wglu_ref[...])
        ya = (y.astype(F32) * _sigmoid(gate)).astype(BF16)
    else:
        (o_ref,) = rest
        ya = ya_ref[...]
    o_ref[...] = h_ref[...] + _dot(ya, wa_ref[...]) + _dot(yb_ref[...], wb_ref[...])


def _out_proj(h, ya, yb, wa, wb, wglu=None, *, tm_target=768, name="out_proj"):
    rows, d = h.shape
    ka, kb = ya.shape[1], yb.shape[1]
    tm = _row_tile(rows, tm_target)
    glu = wglu is not None
    in_specs = [
        pl.BlockSpec((tm, d), lambda i: (i, 0)),
        pl.BlockSpec((tm, ka), lambda i: (i, 0)),
        pl.BlockSpec((tm, kb), lambda i: (i, 0)),
        pl.BlockSpec((ka, d), lambda i: (0, 0)),
        pl.BlockSpec((kb, d), lambda i: (0, 0)),
    ]
    args = [h, ya, yb, wa, wb]
    if glu:
        in_specs.append(pl.BlockSpec((ka, ka), lambda i: (0, 0)))
        args.append(wglu)
    return pl.pallas_call(
        functools.partial(_out_proj_kernel, glu=glu),
        grid=(rows // tm,),
        in_specs=in_specs,
        out_specs=pl.BlockSpec((tm, d), lambda i: (i, 0)),
        out_shape=jax.ShapeDtypeStruct((rows, d), F32),
        compiler_params=_cparams("parallel"),
        name=name,
    )(*args)


def _cmul(ar, ai, br, bi):
    return ar * br - ai * bi, ar * bi + ai * br


def _s5_kernel(u_ref, bb_ref, cm_ref, tneg_ref, tpos_ref, tpos1_ref, d_ref, o_ref, x_s, s_s, *, seq):
    n = S5_BSTATE
    c = S5_CHUNK
    u = u_ref[0]
    x_s[...] = _dot(u.astype(BF16), bb_ref[0])

    sub = S5_SUB

    def tri(size):
        r = lax.broadcasted_iota(jnp.int32, (size, size), 0)
        col = lax.broadcasted_iota(jnp.int32, (size, size), 1)
        return ((r >= col) & (r // sub == col // sub)).astype(BF16)

    def chunk(r0, size, ltri, carry):
        cr, ci = carry
        x = x_s[pl.ds(r0, size), :]
        tn = tneg_ref[0, pl.ds(0, size), :]
        tp = tpos_ref[0, pl.ds(0, size), :]
        tq = tpos1_ref[0, pl.ds(0, sub), :]
        xr, xi = _cmul(x[:, :n], x[:, n:], tn[:, :n], tn[:, n:])
        acc = _dot(ltri, jnp.concatenate([xr, xi], axis=1).astype(BF16))
        pr, pi = _cmul(acc[:, :n], acc[:, n:], tp[:, :n], tp[:, n:])
        srs, sis = [], []
        for k in range(size // sub):
            rows = slice(k * sub, (k + 1) * sub)
            qr, qi = _cmul(tq[:, :n], tq[:, n:], cr, ci)
            srs.append(pr[rows] + qr)
            sis.append(pi[rows] + qi)
            cr, ci = srs[-1][sub - 1:sub, :], sis[-1][sub - 1:sub, :]
        s_s[pl.ds(r0, size), :] = jnp.concatenate(
            [jnp.concatenate(srs, axis=0), jnp.concatenate(sis, axis=0)], axis=1).astype(BF16)
        return cr, ci

    nfull = seq // c
    tail = seq - nfull * c
    ltri = tri(c)
    zero = jnp.zeros((1, n), F32)
    carry = lax.fori_loop(
        0, nfull, lambda i, cy: chunk(pl.multiple_of(i * c, c), c, ltri, cy), (zero, zero))
    if tail:
        chunk(nfull * c, tail, tri(tail), carry)
    y = _dot(s_s[...], cm_ref[0]) + d_ref[0] * u
    o_ref[0] = jax.nn.gelu(y, approximate=True).astype(o_ref.dtype)


def _s5(p3d, tabs):
    bsz, seq, _ = p3d.shape
    n2 = 2 * S5_BSTATE
    tab_spec = pl.BlockSpec((1, S5_CHUNK, n2), lambda b, k: (k, 0, 0))
    return pl.pallas_call(
        functools.partial(_s5_kernel, seq=seq),
        grid=(bsz, S5_BUNDLES),
        in_specs=[
            pl.BlockSpec((1, seq, LANES), lambda b, k: (b, 0, k)),
            pl.BlockSpec((1, LANES, n2), lambda b, k: (k, 0, 0)),
            pl.BlockSpec((1, n2, LANES), lambda b, k: (k, 0, 0)),
            tab_spec, tab_spec, tab_spec,
            pl.BlockSpec((1, 1, LANES), lambda b, k: (k, 0, 0)),
        ],
        out_specs=pl.BlockSpec((1, seq, LANES), lambda b, k: (b, 0, k)),
        out_shape=jax.ShapeDtypeStruct((bsz, seq, S5_WIDTH), BF16),
        scratch_shapes=[pltpu.VMEM((seq, n2), F32), pltpu.VMEM((seq, n2), BF16)],
        compiler_params=_cparams("parallel", "parallel"),
        name="s5",
    )(p3d, tabs["bbar"], tabs["cmat"], tabs["tneg"], tabs["tpos"], tabs["tpos1"], tabs["dskip"])


def _s5_tables(log_dt, a_re, a_im, b_re, b_im, c_re, c_im, d_skip):
    g, p, c = S5_GROUPS, S5_STATE, S5_GROUP
    nb, gb = S5_BUNDLES, S5_BUNDLE_GROUPS
    dt = jnp.exp(log_dt.astype(F32))[:, None]
    ar, ai = a_re.astype(F32), a_im.astype(F32)
    mag = jnp.exp(dt * ar)
    lam_r, lam_i = mag * jnp.cos(dt * ai), mag * jnp.sin(dt * ai)
    den = ar * ar + ai * ai
    coef_r = ((lam_r - 1.0) * ar + lam_i * ai) / den
    coef_i = (lam_i * ar - (lam_r - 1.0) * ai) / den
    br, bi = b_re.astype(F32), b_im.astype(F32)
    bbar_r = coef_r[..., None] * br - coef_i[..., None] * bi
    bbar_i = coef_r[..., None] * bi + coef_i[..., None] * br
    eye = jnp.eye(gb, dtype=F32)

    def block_in(m):
        m = m.reshape(nb, gb, p, c)
        return jnp.einsum("kgpc,gh->kgchp", m, eye).reshape(nb, gb * c, gb * p)

    def block_out(m):
        m = m.reshape(nb, gb, c, p)
        return jnp.einsum("kgcp,gh->kgphc", m, eye).reshape(nb, gb * p, gb * c)

    bbar = jnp.concatenate([block_in(bbar_r), block_in(bbar_i)], axis=2).astype(BF16)
    cmat = jnp.concatenate([block_out(c_re.astype(F32)), -block_out(c_im.astype(F32))], axis=1).astype(BF16)

    def power_table(offset, sign):
        e = sign * ((jnp.arange(S5_CHUNK) % S5_SUB).astype(F32) + offset)[:, None, None]
        m = jnp.exp(e * (dt * ar)[None])
        re = (m * jnp.cos(e * (dt * ai)[None])).reshape(S5_CHUNK, nb, gb * p)
        im = (m * jnp.sin(e * (dt * ai)[None])).reshape(S5_CHUNK, nb, gb * p)
        return jnp.transpose(jnp.concatenate([re, im], axis=2), (1, 0, 2))

    return {
        "bbar": bbar, "cmat": cmat,
        "tneg": power_table(-S5_SUB / 2, -1.0), "tpos": power_table(-S5_SUB / 2, 1.0),
        "tpos1": power_table(1.0, 1.0),
        "dskip": d_skip.astype(F32).reshape(nb, 1, gb * c),
    }


def _query_blocks(seq, qb, kb):
    assert qb % kb == 0 and seq >= qb and seq % qb <= kb
    nloop = seq // qb - 1
    last_r0 = nloop * qb
    last_rows = seq - last_r0
    diag = [(d * kb, kb) for d in range(qb // kb)]
    last_diag = diag + ([(qb, last_rows - qb)] if last_rows > qb else [])
    return nloop, last_r0, last_rows, diag, last_diag


def _softmax_steps(ss, vs, carries):
    ms, ls, accs = zip(*carries)
    m_news = [jnp.maximum(m, jnp.max(s, axis=-1, keepdims=True)) for m, s in zip(ms, ss)]
    ps = [jnp.exp(s - mn) for s, mn in zip(ss, m_news)]
    alphas = [jnp.exp(m - mn) for m, mn in zip(ms, m_news)]
    ls = [a * l + jnp.sum(p, axis=-1, keepdims=True) for a, l, p in zip(alphas, ls, ps)]
    pvs = [_dot(p.astype(BF16), v) for p, v in zip(ps, vs)]
    accs = [a * acc + pv for a, acc, pv in zip(alphas, accs, pvs)]
    return tuple(zip(m_news, ls, accs))


def _mla_kernel(cq_ref, ckv_ref, kr_ref, gcq_ref, gckv_ref, wq_ref, wkv_ref, gq_ref, gkn_ref, gkr_ref,
                cs_ref, o_ref, cqn_s, ckvn_s, krope_s, krss_s, q_s, k_s, v_s, *, seq):
    qb, kb = ATT_Q_BLOCK, ATT_BLOCK
    heads = range(q_s.shape[0])
    lane = lax.broadcasted_iota(jnp.int32, (seq, LANES), 1)
    first_half = lane < MLA_ROPE
    cs = cs_ref[...]

    def rope_part(t):
        t = t * cs
        t = t + pltpu.roll(t, MLA_ROPE, axis=1)
        return jnp.where(first_half, t, 0.0)

    @pl.when(pl.program_id(1) == 0)
    def _():
        cqn_s[...] = _rmsnorm(cq_ref[0], gcq_ref[...]).astype(BF16)
        ckvn_s[...] = _rmsnorm(ckv_ref[0], gckv_ref[...]).astype(BF16)
        kr = kr_ref[0]
        krope_s[...] = rope_part(kr * gkr_ref[...])
        ss = jnp.sum(jnp.where(first_half, kr * kr, 0.0), axis=-1, keepdims=True)
        krss_s[...] = jnp.broadcast_to(ss, (seq, LANES))

    gq = gq_ref[...]
    xqs = [_dot(cqn_s[...], wq_ref[h]) for h in heads]
    sqs = [lax.rsqrt(jnp.sum(x[:, :LANES] * x[:, :LANES] + jnp.where(first_half, x[:, LANES:] * x[:, LANES:], 0.0),
                             axis=-1, keepdims=True) * (1.0 / MLA_QK) + EPS) * (MLA_QK ** -0.5) for x in xqs]
    for h, x, sq in zip(heads, xqs, sqs):
        q_s[h] = jnp.concatenate(
            [x[:, :LANES] * (sq * gq[:, :LANES]), rope_part(x[:, LANES:] * (sq * gq[:, LANES:]))],
            axis=1).astype(BF16)
    xkvs = [_dot(ckvn_s[...], wkv_ref[h]) for h in heads]
    sks = [lax.rsqrt((jnp.sum(x[:, :LANES] * x[:, :LANES], axis=-1, keepdims=True) + krss_s[...])
                     * (1.0 / MLA_QK) + EPS) for x in xkvs]
    for h, x, sk in zip(heads, xkvs, sks):
        k_s[h] = jnp.concatenate([x[:, :LANES] * (sk * gkn_ref[...]), krope_s[...] * sk], axis=1).astype(BF16)
        v_s[h] = x[:, LANES:].astype(BF16)

    def q_block(r0, rows, nprev, diag):
        qs = [q_s[h, pl.ds(r0, rows), :] for h in heads]

        def scores(c0, size):
            return ([_dot_nt(q, k_s[h, pl.ds(c0, size), :]) for h, q in zip(heads, qs)],
                    [v_s[h, pl.ds(c0, size), :] for h in heads])

        def body(j, carry):
            return _softmax_steps(*scores(pl.multiple_of(j * kb, kb), kb), carry)

        init = (jnp.full((rows, 1), NEG_BIG, F32), jnp.zeros((rows, 1), F32), jnp.zeros((rows, MLA_V), F32))
        carry = lax.fori_loop(0, nprev, body, tuple(init for _ in heads))
        for off, size in diag:
            ss, vs = scores(r0 + off, size)
            r = lax.broadcasted_iota(jnp.int32, (rows, size), 0)
            c = lax.broadcasted_iota(jnp.int32, (rows, size), 1)
            carry = _softmax_steps([jnp.where(c + off <= r, s, NEG_BIG) for s in ss], vs, carry)
        for h, (_, l, acc) in zip(heads, carry):
            o_ref[0, pl.ds(r0, rows), h * MLA_V:(h + 1) * MLA_V] = (acc / l).astype(o_ref.dtype)

    nloop, last_r0, last_rows, diag, last_diag = _query_blocks(seq, qb, kb)

    def outer(i, _):
        q_block(pl.multiple_of(i * qb, qb), qb, i * (qb // kb), diag)
        return 0

    lax.fori_loop(0, nloop, outer, 0)
    q_block(last_r0, last_rows, last_r0 // kb, last_diag)


def _mla(p3d, g_cq, g_ckv, wq, wkv, gq, gkn, gkr, cs):
    bsz, seq, _ = p3d.shape
    hp = ATT_HEADS_PER_STEP
    full = lambda shape: pl.BlockSpec(shape, lambda b, g: (0,) * len(shape))
    return pl.pallas_call(
        functools.partial(_mla_kernel, seq=seq),
        grid=(bsz, MLA_HEADS // hp),
        in_specs=[
            pl.BlockSpec((1, seq, MLA_Q_RANK), lambda b, g: (b, 0, S5_WIDTH // MLA_Q_RANK)),
            pl.BlockSpec((1, seq, MLA_KV_RANK), lambda b, g: (b, 0, (S5_WIDTH + MLA_Q_RANK) // MLA_KV_RANK)),
            pl.BlockSpec((1, seq, LANES), lambda b, g: (b, 0, (S5_WIDTH + MLA_Q_RANK + MLA_KV_RANK) // LANES)),
            full((1, MLA_Q_RANK)), full((1, MLA_KV_RANK)),
            pl.BlockSpec((hp, MLA_Q_RANK, 2 * LANES), lambda b, g: (g, 0, 0)),
            pl.BlockSpec((hp, MLA_KV_RANK, 2 * LANES), lambda b, g: (g, 0, 0)),
            full((1, 2 * LANES)), full((1, LANES)), full((1, LANES)),
            full((seq, LANES)),
        ],
        out_specs=pl.BlockSpec((1, seq, hp * MLA_V), lambda b, g: (b, 0, g)),
        out_shape=jax.ShapeDtypeStruct((bsz, seq, MLA_HEADS * MLA_V), BF16),
        scratch_shapes=[pltpu.VMEM((seq, MLA_Q_RANK), BF16), pltpu.VMEM((seq, MLA_KV_RANK), BF16),
                        pltpu.VMEM((seq, LANES), F32), pltpu.VMEM((seq, LANES), F32),
                        pltpu.VMEM((hp, seq, 2 * LANES), BF16), pltpu.VMEM((hp, seq, 2 * LANES), BF16),
                        pltpu.VMEM((hp, seq, MLA_V), BF16)],
        compiler_params=_cparams("parallel", "arbitrary"),
        name="mla",
    )(p3d, p3d, p3d, g_cq, g_ckv, wq, wkv, gq, gkn, gkr, cs)


def _rot_cols(w):
    half = MLA_ROPE // 2
    return jnp.concatenate([-w[..., half:], w[..., :half]], axis=-1)


def _swap_halves(g):
    half = MLA_ROPE // 2
    return jnp.concatenate([g[..., half:], g[..., :half]], axis=-1)


def _mla_params(w_uq, w_ukv, g_q, g_k, seq):
    wq = w_uq.reshape(MLA_Q_RANK, MLA_HEADS, MLA_QK)
    wq = jnp.concatenate([wq, _rot_cols(wq[..., MLA_NOPE:])], axis=-1)
    wq = jnp.transpose(wq, (1, 0, 2)).astype(BF16)
    wkv = jnp.transpose(w_ukv.reshape(MLA_KV_RANK, MLA_HEADS, MLA_NOPE + MLA_V), (1, 0, 2)).astype(BF16)
    gq = jnp.concatenate([g_q, _swap_halves(g_q[MLA_NOPE:])]).reshape(1, 2 * LANES).astype(F32)
    gkn = g_k[:MLA_NOPE].reshape(1, LANES).astype(F32)
    gkr = jnp.concatenate([g_k[MLA_NOPE:], _swap_halves(g_k[MLA_NOPE:])]).reshape(1, LANES).astype(F32)
    half = MLA_ROPE // 2
    inv = ROPE_THETA ** (-jnp.arange(half, dtype=F32) / half)
    ang = jnp.arange(seq, dtype=F32)[:, None] * inv[None, :]
    cs = jnp.concatenate([jnp.cos(ang), jnp.cos(ang), jnp.sin(ang), jnp.sin(ang)], axis=1)
    return wq, wkv, gq, gkn, gkr, cs


def _split2(x):
    hi = x.astype(BF16)
    return hi, (x - hi.astype(F32)).astype(BF16)


def _split3(x):
    hi = x.astype(BF16)
    r = x - hi.astype(F32)
    mid = r.astype(BF16)
    return hi, mid, (r - mid.astype(F32)).astype(BF16)


def _sb_kernel(q_ref, k_ref, v_ref, o_ref, q_s, k_s, v_s, *, seq):
    qb, blk = ATT_Q_BLOCK, SB_BLOCK
    heads = range(q_s.shape[0])
    for h in heads:
        cols = slice(h * SB_DIM, (h + 1) * SB_DIM)
        q_s[h] = (q_ref[0, :, cols] * (SB_DIM ** -0.5)).astype(BF16)
        k_s[h] = k_ref[0, :, cols].astype(BF16)
        v_s[h] = v_ref[0, :, cols].astype(BF16)

    def later_keys(size):
        r = lax.broadcasted_iota(jnp.int32, (size, size), 0)
        c = lax.broadcasted_iota(jnp.int32, (size, size), 1)
        return (r > c).astype(BF16)

    def tiles(qs, c0, size, u_mat, strict, carries):
        tails, accs = zip(*carries)
        zs = [_dot_nt(q, k_s[h, pl.ds(c0, size), :]) for h, q in zip(heads, qs)]
        log_keeps = [-(jnp.maximum(z, 0.0) + jnp.log(1.0 + jnp.exp(-jnp.abs(z)))) for z in zs]
        if strict is not None:
            log_keeps = [jnp.where(strict, lk, 0.0) for lk in log_keeps]
        splits = [_split2(lk) for lk in log_keeps]
        inners = [_dot(hi, u_mat) + _dot(lo, u_mat) for hi, lo in splits]
        ws = [jnp.exp(z + lk + inner + tail) for z, lk, inner, tail in zip(zs, log_keeps, inners, tails)]
        if strict is not None:
            ws = [jnp.where(strict, w, 0.0) for w in ws]
        accs = [acc + _dot(w.astype(BF16), v_s[h, pl.ds(c0, size), :]) for h, w, acc in zip(heads, ws, accs)]
        tails = [tail + jnp.sum(lk, axis=-1, keepdims=True) for tail, lk in zip(tails, log_keeps)]
        return tuple(zip(tails, accs))

    u_blk = later_keys(blk)

    def q_block(r0, rows, nprev, diag):
        qs = [q_s[h, pl.ds(r0, rows), :] for h in heads]
        carry = tuple((jnp.zeros((rows, 1), F32), jnp.zeros((rows, SB_DIM), F32)) for _ in heads)
        for off, size, u_mat in reversed(diag):
            r = lax.broadcasted_iota(jnp.int32, (rows, size), 0)
            c = lax.broadcasted_iota(jnp.int32, (rows, size), 1)
            carry = tiles(qs, r0 + off, size, u_mat, c + off < r, carry)

        def live(cy):
            tail_max = functools.reduce(jnp.maximum, [tail for tail, _ in cy])
            return (jnp.max(tail_max) > SB_DEAD_TAIL).astype(jnp.int32)

        def body(state):
            jj, _, cy = state
            c0 = pl.multiple_of((nprev - 1 - jj) * blk, blk)
            cy = tiles(qs, c0, blk, u_blk, None, cy)
            return jj + 1, live(cy), cy

        _, _, carry = lax.while_loop(lambda s: (s[0] < nprev) & (s[1] > 0), body,
                                     (jnp.int32(0), live(carry), carry))
        for h, (_, acc) in zip(heads, carry):
            o_ref[0, pl.ds(r0, rows), h * SB_DIM:(h + 1) * SB_DIM] = acc.astype(o_ref.dtype)

    nloop, last_r0, last_rows, diag, last_diag = _query_blocks(seq, qb, blk)
    u_of = lambda size: u_blk if size == blk else later_keys(size)

    def outer(i, _):
        q_block(pl.multiple_of(i * qb, qb), qb, i * (qb // blk), [(o, s, u_of(s)) for o, s in diag])
        return 0

    lax.fori_loop(0, nloop, outer, 0)
    q_block(last_r0, last_rows, last_r0 // blk, [(o, s, u_of(s)) for o, s in last_diag])


def _stick_breaking(p3d):
    bsz, seq, _ = p3d.shape
    hp = ATT_HEADS_PER_STEP
    groups = SB_HEADS // hp
    spec = lambda off: pl.BlockSpec((1, seq, hp * SB_DIM), lambda b, g: (b, 0, off + g))
    return pl.pallas_call(
        functools.partial(_sb_kernel, seq=seq),
        grid=(bsz, groups),
        in_specs=[spec(0), spec(groups), spec(2 * groups)],
        out_specs=pl.BlockSpec((1, seq, hp * SB_DIM), lambda b, g: (b, 0, g)),
        out_shape=jax.ShapeDtypeStruct((bsz, seq, SB_W), BF16),
        scratch_shapes=[pltpu.VMEM((hp, seq, SB_DIM), BF16)] * 3,
        compiler_params=_cparams("parallel", "parallel"),
        name="stick_breaking",
    )(p3d, p3d, p3d)


def _gdn_kernel(q_ref, k_ref, v_ref, z_ref, ab_ref, cwq_ref, cwk_ref, cwv_ref, gate_ref, gout_ref, o_ref,
                pad_s, q_s, k_s, v_s, g_s, b_s, o_s, u_s, w_s, a_s, qd_s, kdt_s, gl_s, *, seq):
    c = GDN_CHUNK
    hp = q_s.shape[0]
    nchunks = pl.cdiv(seq, c)
    padded = nchunks * c
    pad_s[pl.ds(0, 8), :] = jnp.zeros((8, LANES), F32)

    def conv_silu(x_ref, w_ref, cols):
        pad_s[pl.ds(8, seq), :] = x_ref[0, :, cols]
        w = w_ref[:, cols]
        y = w[GDN_CONV - 1:GDN_CONV, :] * pad_s[pl.ds(8, seq), :]
        for j in range(GDN_CONV - 1):
            y = y + w[j:j + 1, :] * pad_s[pl.ds(8 - (GDN_CONV - 1) + j, seq), :]
        return y * _sigmoid(y)

    def l2norm(x):
        return x * lax.rsqrt(jnp.sum(x * x, axis=-1, keepdims=True) + EPS)

    ab = ab_ref[0]
    lane = lax.broadcasted_iota(jnp.int32, (seq, LANES), 1)
    gate = gate_ref[...]
    t = ab + gate[1:2, :]
    g_all = -jnp.exp(gate[0:1, :]) * (jnp.maximum(t, 0.0) + jnp.log(1.0 + jnp.exp(-jnp.abs(t))))
    beta_all = _sigmoid(ab)
    zeros_tail = jnp.zeros((padded - seq, LANES), F32)

    def prepare(hd):
        cols = slice(hd * LANES, (hd + 1) * LANES)
        head = pl.program_id(1) * hp + hd
        if padded > seq:
            for ref in (q_s, k_s, v_s, g_s, b_s):
                ref[hd, pl.ds(seq, padded - seq), :] = zeros_tail
        q_s[hd, pl.ds(0, seq), :] = l2norm(conv_silu(q_ref, cwq_ref, cols)) * (GDN_DK ** -0.5)
        k_s[hd, pl.ds(0, seq), :] = l2norm(conv_silu(k_ref, cwk_ref, cols))
        v_s[hd, pl.ds(0, seq), :] = conv_silu(v_ref, cwv_ref, cols)
        g_col = jnp.sum(jnp.where(lane == head, g_all, 0.0), axis=-1, keepdims=True)
        b_col = jnp.sum(jnp.where(lane == head + GDN_HEADS, beta_all, 0.0), axis=-1, keepdims=True)
        g_s[hd, pl.ds(0, seq), :] = jnp.broadcast_to(g_col, (seq, LANES))
        b_s[hd, pl.ds(0, seq), :] = jnp.broadcast_to(b_col, (seq, LANES))

    r = lax.broadcasted_iota(jnp.int32, (c, c), 0)
    col = lax.broadcasted_iota(jnp.int32, (c, c), 1)
    lower = r >= col
    strict = r > col
    incl = lower.astype(BF16)
    eye = (r == col).astype(F32)
    level_masks = [((r >> l) == (col >> l)) & ((r >> (l - 1)) != (col >> (l - 1))) & strict
                   for l in range(1, int(math.log2(c)) + 1)]

    def local(hd, first, count):
        ids = [first + t for t in range(count)]
        rows = [pl.ds(_aligned(i * c, c), c) for i in ids]
        ks = [k_s[hd, r, :] for r in rows]
        betas = [b_s[hd, r, :] for r in rows]
        gbs = [g_s[hd, r, :] for r in rows]
        sums = [_dot(incl, jnp.concatenate(_split3(gb) + _split3(jnp.where(strict, gb, 0.0)), axis=1))
                for gb in gbs]
        cums = [s[:, 0:c] + s[:, c:2 * c] + s[:, 2 * c:3 * c] for s in sums]
        decays = [jnp.exp(jnp.where(lower, s[:, 3 * c:4 * c] + s[:, 4 * c:5 * c] + s[:, 5 * c:6 * c], NEG_BIG))
                  for s in sums]
        kbs = [k * beta for k, beta in zip(ks, betas)]
        kbfs = [k.astype(BF16) for k in ks]
        ms = [jnp.where(strict, _dot_nt(kb.astype(BF16), kbf) * decay, 0.0)
              for kb, kbf, decay in zip(kbs, kbfs, decays)]
        invs = [eye.astype(BF16)] * count
        for mask in level_masks:
            ts = [_dot(jnp.where(mask, m, 0.0).astype(BF16), inv).astype(BF16) for m, inv in zip(ms, invs)]
            invs = [(inv.astype(F32) - _dot(inv, t)).astype(BF16) for inv, t in zip(invs, ts)]
        e_cums = [jnp.exp(cum) for cum in cums]
        for r, inv, kb, e_cum in zip(rows, invs, kbs, e_cums):
            w_s[hd, r, :] = _dot(inv, (kb * e_cum).astype(BF16)).astype(BF16)
        for r, inv, beta in zip(rows, invs, betas):
            u_s[hd, r, :] = _dot(inv, (v_s[hd, r, :] * beta).astype(BF16))
        qs = [q_s[hd, r, :] for r in rows]
        for r, q, kbf, decay in zip(rows, qs, kbfs, decays):
            a_s[hd, r, :] = jnp.where(lower, _dot_nt(q.astype(BF16), kbf) * decay, 0.0).astype(BF16)
        for i, r, q, k, cum, e_cum in zip(ids, rows, qs, ks, cums, e_cums):
            qd_s[hd, r, :] = (q * e_cum).astype(BF16)
            last = cum[c - 1:c, :]
            kdt_s[hd, r, :] = (k * jnp.exp(last - cum)).T.astype(BF16)
            gl_s[hd, pl.ds(_aligned(i * 8, 8), 8), :] = jnp.broadcast_to(jnp.exp(last), (8, LANES))

    def recur(i, states):
        rows = pl.ds(pl.multiple_of(i * c, c), c)
        heads = range(hp)
        sbs = [s.astype(BF16) for s in states]
        v_news = [(u_s[hd, rows, :] - _dot(w_s[hd, rows, :], sb)).astype(BF16) for hd, sb in zip(heads, sbs)]
        for hd, sb, v_new in zip(heads, sbs, v_news):
            o_s[hd, rows, :] = _dot(qd_s[hd, rows, :], sb) + _dot(a_s[hd, rows, :], v_new)
        return tuple(s * gl_s[hd, pl.ds(pl.multiple_of(i * 8, 8), 1), :] + _dot(kdt_s[hd, rows, :], v_new)
                     for hd, s, v_new in zip(heads, states, v_news))

    ngroups = max(nchunks // GDN_UNROLL, 1)
    for hd in range(hp):
        prepare(hd)
        for gi in range(ngroups):
            first = gi * GDN_UNROLL
            local(hd, first, GDN_UNROLL if gi < ngroups - 1 else nchunks - first)
    lax.fori_loop(0, nchunks, recur, tuple(jnp.zeros((GDN_DK, GDN_DV), F32) for _ in range(hp)))
    for hd in range(hp):
        cols = slice(hd * LANES, (hd + 1) * LANES)
        z = z_ref[0, :, cols]
        o = _rmsnorm(o_s[hd, pl.ds(0, seq), :], gout_ref[...]) * (z * _sigmoid(z))
        o_ref[0, :, cols] = o.astype(o_ref.dtype)


def _gated_deltanet(p3d, conv_w, gate, g_out):
    bsz, seq, _ = p3d.shape
    hp = GDN_HEADS_PER_STEP
    groups = GDN_HEADS // hp
    off = 3 * SB_HEADS // hp
    spec = lambda o: pl.BlockSpec((1, seq, hp * LANES), lambda b, g: (b, 0, o + g))
    cspec = lambda o: pl.BlockSpec((GDN_CONV, hp * LANES), lambda b, g: (0, o + g))
    padded = pl.cdiv(seq, GDN_CHUNK) * GDN_CHUNK
    return pl.pallas_call(
        functools.partial(_gdn_kernel, seq=seq),
        grid=(bsz, groups),
        in_specs=[
            spec(off), spec(off + groups), spec(off + 2 * groups), spec(off + 3 * groups),
            pl.BlockSpec((1, seq, LANES), lambda b, g: (b, 0, (3 * SB_HEADS + 4 * GDN_HEADS))),
            cspec(0), cspec(groups), cspec(2 * groups),
            pl.BlockSpec((2, LANES), lambda b, g: (0, 0)),
            pl.BlockSpec((1, LANES), lambda b, g: (0, 0)),
        ],
        out_specs=pl.BlockSpec((1, seq, hp * LANES), lambda b, g: (b, 0, g)),
        out_shape=jax.ShapeDtypeStruct((bsz, seq, GDN_HEADS * GDN_DV), BF16),
        scratch_shapes=([pltpu.VMEM((seq + 8, LANES), F32)] + [pltpu.VMEM((hp, padded, LANES), F32)] * 7
                        + [pltpu.VMEM((hp, padded, LANES), BF16)] * 4
                        + [pltpu.VMEM((hp, 8 * padded // GDN_CHUNK, LANES), F32)]),
        compiler_params=_cparams("parallel", "parallel"),
        name="gated_deltanet",
    )(p3d, p3d, p3d, p3d, p3d, conv_w, conv_w, conv_w, gate, g_out)


def _even_mixer(h, bsz, seq, gain, w_in, s5_params, w_glu, g_cq, g_ckv, w_uq, w_ukv, g_q, g_k, w_out):
    d = h.shape[1]
    rope0 = S5_WIDTH + MLA_Q_RANK + MLA_KV_RANK
    w_ext = jnp.concatenate([w_in, _rot_cols(w_in[:, rope0:])], axis=1).astype(BF16)
    p = _norm_matmul(h, gain, w_ext, tn=w_ext.shape[1], name="even_in_proj")
    p3d = p.reshape(bsz, seq, -1)
    y_a = _s5(p3d, _s5_tables(*s5_params))
    wq, wkv, gq, gkn, gkr, cs = _mla_params(w_uq, w_ukv, g_q, g_k, seq)
    y_b = _mla(p3d, g_cq.reshape(1, -1), g_ckv.reshape(1, -1), wq, wkv, gq, gkn, gkr, cs)
    return _out_proj(h, y_a.reshape(-1, S5_WIDTH), y_b.reshape(-1, MLA_HEADS * MLA_V),
                     w_out[:S5_WIDTH].astype(BF16), w_out[S5_WIDTH:].astype(BF16), w_glu.astype(BF16),
                     name="even_out_proj")


def _odd_mixer(h, bsz, seq, gain, w_in, conv_w, a_log, dt_bias, g_out, w_out):
    d = h.shape[1]
    o1 = 3 * SB_W + GDN_QKV
    ab = w_in[:, o1:o1 + 2 * GDN_HEADS]
    w_ext = jnp.concatenate(
        [w_in[:, :o1], w_in[:, o1 + 2 * GDN_HEADS:], ab, jnp.zeros((d, LANES - 2 * GDN_HEADS), w_in.dtype)],
        axis=1).astype(BF16)
    p = _norm_matmul(h, gain, w_ext, tn=w_ext.shape[1] // 3, name="odd_in_proj")
    p3d = p.reshape(bsz, seq, -1)
    y_c = _stick_breaking(p3d)
    pad = jnp.zeros((LANES - GDN_HEADS,), F32)
    gate = jnp.stack([jnp.concatenate([a_log.astype(F32), pad]), jnp.concatenate([dt_bias.astype(F32), pad])])
    y_d = _gated_deltanet(p3d, conv_w.astype(F32), gate, g_out.reshape(1, -1).astype(F32))
    return _out_proj(h, y_c.reshape(-1, SB_W), y_d.reshape(-1, GDN_HEADS * GDN_DV),
                     w_out[:SB_W].astype(BF16), w_out[SB_W:].astype(BF16), name="odd_out_proj")


def kernel(x, meta_tokens, norm_ffn1, w1_gate, w1_up, w1_down, norm_mix, norm_ffn2, w2_gate, w2_up, w2_down, ev_w_in, s5_log_dt, s5_a_re, s5_a_im, s5_b_re, s5_b_im, s5_c_re, s5_c_im, s5_d, s5_w_glu, mla_g_cq, mla_g_ckv, mla_w_uq, mla_w_ukv, mla_g_q, mla_g_k, ev_w_out, od_w_in, gdn_conv, gdn_a_log, gdn_dt_bias, gdn_g_out, od_w_out):
    bsz, _, d = x.shape
    depth = norm_ffn1.shape[0]
    meta = jnp.broadcast_to(meta_tokens[None].astype(x.dtype), (bsz, N_META, d))
    h3 = jnp.concatenate([meta, x], axis=1)
    seq = h3.shape[1]
    h = h3.reshape(bsz * seq, d)
    for l in range(depth):
        i = l // 2
        h = _ffn(h, norm_ffn1[l], w1_gate[l].astype(BF16), w1_up[l].astype(BF16), w1_down[l].astype(BF16))
        if l % 2 == 0:
            s5_params = (s5_log_dt[i], s5_a_re[i], s5_a_im[i], s5_b_re[i], s5_b_im[i], s5_c_re[i],
                         s5_c_im[i], s5_d[i])
            h = _even_mixer(h, bsz, seq, norm_mix[l], ev_w_in[i], s5_params, s5_w_glu[i], mla_g_cq[i],
                            mla_g_ckv[i], mla_w_uq[i], mla_w_ukv[i], mla_g_q[i], mla_g_k[i], ev_w_out[i])
        else:
            h = _odd_mixer(h, bsz, seq, norm_mix[l], od_w_in[i], gdn_conv[i], gdn_a_log[i], gdn_dt_bias[i],
                           gdn_g_out[i], od_w_out[i])
        h = _ffn(h, norm_ffn2[l], w2_gate[l].astype(BF16), w2_up[l].astype(BF16), w2_down[l].astype(BF16))
    return h.reshape(bsz, seq, d)[:, N_META:]
```

```python
import functools
import math

import jax
import jax.numpy as jnp
from jax import lax
from jax.experimental import pallas as pl
from jax.experimental.pallas import tpu as pltpu

F32 = jnp.float32
BF16 = jnp.bfloat16
EPS = 1e-6
N_META = 16
LANES = 128
VMEM_LIMIT_BYTES = 56 * 2**20
NEG_BIG = -1e30

S5_GROUP, S5_STATE, S5_GROUPS = 16, 64, 32
S5_WIDTH = S5_GROUPS * S5_GROUP
S5_BUNDLE_GROUPS = LANES // S5_GROUP
S5_BUNDLES = S5_GROUPS // S5_BUNDLE_GROUPS
S5_BSTATE = S5_BUNDLE_GROUPS * S5_STATE
S5_CHUNK = 128
S5_SUB = 16
MLA_HEADS, MLA_Q_RANK, MLA_KV_RANK = 8, 512, 256
MLA_NOPE, MLA_ROPE, MLA_V = 128, 64, 128
MLA_QK = MLA_NOPE + MLA_ROPE
ROPE_THETA = 10000.0
ATT_BLOCK = 512
SB_BLOCK = 256
SB_DEAD_TAIL = -104.0
ATT_Q_BLOCK = 512
ATT_HEADS_PER_STEP = 2
SB_HEADS, SB_DIM = 8, 128
SB_W = SB_HEADS * SB_DIM
GDN_HEADS, GDN_DK, GDN_DV, GDN_CONV = 8, 128, 128, 4
GDN_CHUNK = 128
GDN_UNROLL = 17
GDN_HEADS_PER_STEP = 2
GDN_QKV = GDN_HEADS * (2 * GDN_DK + GDN_DV)


def _cparams(*sem):
    return pltpu.CompilerParams(dimension_semantics=sem, vmem_limit_bytes=VMEM_LIMIT_BYTES)


def _row_tile(rows, target):
    best = None
    for t in range(16, min(rows, target) + 1, 16):
        if rows % t == 0:
            best = t
    assert best is not None, rows
    return best


def _rmsnorm(x, g):
    return x * lax.rsqrt(jnp.mean(x * x, axis=-1, keepdims=True) + EPS) * g


def _sigmoid(x):
    return 1.0 / (1.0 + jnp.exp(-x))


def _dot(a, b):
    return jnp.dot(a, b, preferred_element_type=F32)


def _aligned(x, m):
    return x if isinstance(x, int) else pl.multiple_of(x, m)


def _dot_nt(a, b):
    return lax.dot_general(a, b, (((1,), (1,)), ((), ())), preferred_element_type=F32)


def _ffn_kernel(x_ref, g_ref, wg_ref, wu_ref, wd_ref, o_ref, xn_ref):
    j = pl.program_id(1)

    @pl.when(j == 0)
    def _():
        xn_ref[...] = _rmsnorm(x_ref[...], g_ref[...]).astype(BF16)
        o_ref[...] = jnp.zeros_like(o_ref)

    xn = xn_ref[...]
    gate = _dot(xn, wg_ref[...])
    up = _dot(xn, wu_ref[...])
    act = (gate * _sigmoid(gate) * up).astype(BF16)
    o_ref[...] += _dot(act, wd_ref[...])

    @pl.when(j == pl.num_programs(1) - 1)
    def _():
        o_ref[...] = x_ref[...] + 0.5 * o_ref[...]


def _ffn(h, gain, wg, wu, wd, *, tm_target=768, tf=512):
    rows, d = h.shape
    f = wg.shape[1]
    tm = _row_tile(rows, tm_target)
    assert f % tf == 0
    return pl.pallas_call(
        _ffn_kernel,
        grid=(rows // tm, f // tf),
        in_specs=[
            pl.BlockSpec((tm, d), lambda i, j: (i, 0)),
            pl.BlockSpec((1, d), lambda i, j: (0, 0)),
            pl.BlockSpec((d, tf), lambda i, j: (0, j)),
            pl.BlockSpec((d, tf), lambda i, j: (0, j)),
            pl.BlockSpec((tf, d), lambda i, j: (j, 0)),
        ],
        out_specs=pl.BlockSpec((tm, d), lambda i, j: (i, 0)),
        out_shape=jax.ShapeDtypeStruct((rows, d), F32),
        scratch_shapes=[pltpu.VMEM((tm, d), BF16)],
        compiler_params=_cparams("parallel", "arbitrary"),
        name="ffn",
    )(h, gain.reshape(1, d), wg, wu, wd)


def _norm_matmul_kernel(x_ref, g_ref, w_ref, o_ref):
    xn = _rmsnorm(x_ref[...], g_ref[...]).astype(BF16)
    o_ref[...] = _dot(xn, w_ref[...]).astype(o_ref.dtype)


def _norm_matmul(h, gain, w, *, tn, tm_target=768, name="norm_matmul"):
    rows, d = h.shape
    n = w.shape[1]
    tm = _row_tile(rows, tm_target)
    assert n % tn == 0
    return pl.pallas_call(
        _norm_matmul_kernel,
        grid=(n // tn, rows // tm),
        in_specs=[
            pl.BlockSpec((tm, d), lambda j, i: (i, 0)),
            pl.BlockSpec((1, d), lambda j, i: (0, 0)),
            pl.BlockSpec((d, tn), lambda j, i: (0, j), pipeline_mode=pl.Buffered(1)),
        ],
        out_specs=pl.BlockSpec((tm, tn), lambda j, i: (i, j)),
        out_shape=jax.ShapeDtypeStruct((rows, n), F32),
        compiler_params=_cparams("parallel", "parallel"),
        name=name,
    )(h, gain.reshape(1, d), w)


def _out_proj_kernel(h_ref, ya_ref, yb_ref, wa_ref, wb_ref, *rest, glu):
    if glu:
        wglu_ref, o_ref = rest
        y = ya_ref[...]
        gate = _dot(y, wglu_ref[...])
        ya = (y.astype(F32) * _sigmoid(gate)).astype(BF16)
    else:
        (o_ref,) = rest
        ya = ya_ref[...]
    o_ref[...] = h_ref[...] + _dot(ya, wa_ref[...]) + _dot(yb_ref[...], wb_ref[...])


def _out_proj(h, ya, yb, wa, wb, wglu=None, *, tm_target=768, name="out_proj"):
    rows, d = h.shape
    ka, kb = ya.shape[1], yb.shape[1]
    tm = _row_tile(rows, tm_target)
    glu = wglu is not None
    in_specs = [
        pl.BlockSpec((tm, d), lambda i: (i, 0)),
        pl.BlockSpec((tm, ka), lambda i: (i, 0)),
        pl.BlockSpec((tm, kb), lambda i: (i, 0)),
        pl.BlockSpec((ka, d), lambda i: (0, 0)),
        pl.BlockSpec((kb, d), lambda i: (0, 0)),
    ]
    args = [h, ya, yb, wa, wb]
    if glu:
        in_specs.append(pl.BlockSpec((ka, ka), lambda i: (0, 0)))
        args.append(wglu)
    return pl.pallas_call(
        functools.partial(_out_proj_kernel, glu=glu),
        grid=(rows // tm,),
        in_specs=in_specs,
        out_specs=pl.BlockSpec((tm, d), lambda i: (i, 0)),
        out_shape=jax.ShapeDtypeStruct((rows, d), F32),
        compiler_params=_cparams("parallel"),
        name=name,
    )(*args)


def _cmul(ar, ai, br, bi):
    return ar * br - ai * bi, ar * bi + ai * br


def _s5_kernel(u_ref, bb_ref, cm_ref, tneg_ref, tpos_ref, tpos1_ref, d_ref, o_ref, x_s, s_s, *, seq):
    n = S5_BSTATE
    c = S5_CHUNK
    u = u_ref[0]
    x_s[...] = _dot(u.astype(BF16), bb_ref[0])

    sub = S5_SUB

    def tri(size):
        r = lax.broadcasted_iota(jnp.int32, (size, size), 0)
        col = lax.broadcasted_iota(jnp.int32, (size, size), 1)
        return ((r >= col) & (r // sub == col // sub)).astype(BF16)

    def chunk(r0, size, ltri, carry):
        cr, ci = carry
        x = x_s[pl.ds(r0, size), :]
        tn = tneg_ref[0, pl.ds(0, size), :]
        tp = tpos_ref[0, pl.ds(0, size), :]
        tq = tpos1_ref[0, pl.ds(0, sub), :]
        xr, xi = _cmul(x[:, :n], x[:, n:], tn[:, :n], tn[:, n:])
        acc = _dot(ltri, jnp.concatenate([xr, xi], axis=1).astype(BF16))
        pr, pi = _cmul(acc[:, :n], acc[:, n:], tp[:, :n], tp[:, n:])
        srs, sis = [], []
        for k in range(size // sub):
            rows = slice(k * sub, (k + 1) * sub)
            qr, qi = _cmul(tq[:, :n], tq[:, n:], cr, ci)
            srs.append(pr[rows] + qr)
            sis.append(pi[rows] + qi)
            cr, ci = srs[-1][sub - 1:sub, :], sis[-1][sub - 1:sub, :]
        s_s[pl.ds(r0, size), :] = jnp.concatenate(
            [jnp.concatenate(srs, axis=0), jnp.concatenate(sis, axis=0)], axis=1).astype(BF16)
        return cr, ci

    nfull = seq // c
    tail = seq - nfull * c
    ltri = tri(c)
    zero = jnp.zeros((1, n), F32)
    carry = lax.fori_loop(
        0, nfull, lambda i, cy: chunk(pl.multiple_of(i * c, c), c, ltri, cy), (zero, zero))
    if tail:
        chunk(nfull * c, tail, tri(tail), carry)
    y = _dot(s_s[...], cm_ref[0]) + d_ref[0] * u
    o_ref[0] = jax.nn.gelu(y, approximate=True).astype(o_ref.dtype)


def _s5(p3d, tabs):
    bsz, seq, _ = p3d.shape
    n2 = 2 * S5_BSTATE
    tab_spec = pl.BlockSpec((1, S5_CHUNK, n2), lambda b, k: (k, 0, 0))
    return pl.pallas_call(
        functools.partial(_s5_kernel, seq=seq),
        grid=(bsz, S5_BUNDLES),
        in_specs=[
            pl.BlockSpec((1, seq, LANES), lambda b, k: (b, 0, k)),
            pl.BlockSpec((1, LANES, n2), lambda b, k: (k, 0, 0)),
            pl.BlockSpec((1, n2, LANES), lambda b, k: (k, 0, 0)),
            tab_spec, tab_spec, tab_spec,
            pl.BlockSpec((1, 1, LANES), lambda b, k: (k, 0, 0)),
        ],
        out_specs=pl.BlockSpec((1, seq, LANES), lambda b, k: (b, 0, k)),
        out_shape=jax.ShapeDtypeStruct((bsz, seq, S5_WIDTH), BF16),
        scratch_shapes=[pltpu.VMEM((seq, n2), F32), pltpu.VMEM((seq, n2), BF16)],
        compiler_params=_cparams("parallel", "parallel"),
        name="s5",
    )(p3d, tabs["bbar"], tabs["cmat"], tabs["tneg"], tabs["tpos"], tabs["tpos1"], tabs["dskip"])


def _s5_tables(log_dt, a_re, a_im, b_re, b_im, c_re, c_im, d_skip):
    g, p, c = S5_GROUPS, S5_STATE, S5_GROUP
    nb, gb = S5_BUNDLES, S5_BUNDLE_GROUPS
    dt = jnp.exp(log_dt.astype(F32))[:, None]
    ar, ai = a_re.astype(F32), a_im.astype(F32)
    mag = jnp.exp(dt * ar)
    lam_r, lam_i = mag * jnp.cos(dt * ai), mag * jnp.sin(dt * ai)
    den = ar * ar + ai * ai
    coef_r = ((lam_r - 1.0) * ar + lam_i * ai) / den
    coef_i = (lam_i * ar - (lam_r - 1.0) * ai) / den
    br, bi = b_re.astype(F32), b_im.astype(F32)
    bbar_r = coef_r[..., None] * br - coef_i[..., None] * bi
    bbar_i = coef_r[..., None] * bi + coef_i[..., None] * br
    eye = jnp.eye(gb, dtype=F32)

    def block_in(m):
        m = m.reshape(nb, gb, p, c)
        return jnp.einsum("kgpc,gh->kgchp", m, eye).reshape(nb, gb * c, gb * p)

    def block_out(m):
        m = m.reshape(nb, gb, c, p)
        return jnp.einsum("kgcp,gh->kgphc", m, eye).reshape(nb, gb * p, gb * c)

    bbar = jnp.concatenate([block_in(bbar_r), block_in(bbar_i)], axis=2).astype(BF16)
    cmat = jnp.concatenate([block_out(c_re.astype(F32)), -block_out(c_im.astype(F32))], axis=1).astype(BF16)

    def power_table(offset, sign):
        e = sign * ((jnp.arange(S5_CHUNK) % S5_SUB).astype(F32) + offset)[:, None, None]
        m = jnp.exp(e * (dt * ar)[None])
        re = (m * jnp.cos(e * (dt * ai)[None])).reshape(S5_CHUNK, nb, gb * p)
        im = (m * jnp.sin(e * (dt * ai)[None])).reshape(S5_CHUNK, nb, gb * p)
        return jnp.transpose(jnp.concatenate([re, im], axis=2), (1, 0, 2))

    return {
        "bbar": bbar, "cmat": cmat,
        "tneg": power_table(-S5_SUB / 2, -1.0), "tpos": power_table(-S5_SUB / 2, 1.0),
        "tpos1": power_table(1.0, 1.0),
        "dskip": d_skip.astype(F32).reshape(nb, 1, gb * c),
    }


def _query_blocks(seq, qb, kb):
    assert qb % kb == 0 and seq >= qb and seq % qb <= kb
    nloop = seq // qb - 1
    last_r0 = nloop * qb
    last_rows = seq - last_r0
    diag = [(d * kb, kb) for d in range(qb // kb)]
    last_diag = diag + ([(qb, last_rows - qb)] if last_rows > qb else [])
    return nloop, last_r0, last_rows, diag, last_diag


def _softmax_steps(ss, vs, carries):
    ms, ls, accs = zip(*carries)
    m_news = [jnp.maximum(m, jnp.max(s, axis=-1, keepdims=True)) for m, s in zip(ms, ss)]
    ps = [jnp.exp(s - mn) for s, mn in zip(ss, m_news)]
    alphas = [jnp.exp(m - mn) for m, mn in zip(ms, m_news)]
    ls = [a * l + jnp.sum(p, axis=-1, keepdims=True) for a, l, p in zip(alphas, ls, ps)]
    pvs = [_dot(p.astype(BF16), v) for p, v in zip(ps, vs)]
    accs = [a * acc + pv for a, acc, pv in zip(alphas, accs, pvs)]
    return tuple(zip(m_news, ls, accs))


def _mla_kernel(cq_ref, ckv_ref, kr_ref, gcq_ref, gckv_ref, wq_ref, wkv_ref, gq_ref, gkn_ref, gkr_ref,
                cs_ref, o_ref, cqn_s, ckvn_s, krope_s, krss_s, q_s, k_s, v_s, *, seq):
    qb, kb = ATT_Q_BLOCK, ATT_BLOCK
    heads = range(q_s.shape[0])
    lane = lax.broadcasted_iota(jnp.int32, (seq, LANES), 1)
    first_half = lane < MLA_ROPE
    cs = cs_ref[...]

    def rope_part(t):
        t = t * cs
        t = t + pltpu.roll(t, MLA_ROPE, axis=1)
        return jnp.where(first_half, t, 0.0)

    @pl.when(pl.program_id(1) == 0)
    def _():
        cqn_s[...] = _rmsnorm(cq_ref[0], gcq_ref[...]).astype(BF16)
        ckvn_s[...] = _rmsnorm(ckv_ref[0], gckv_ref[...]).astype(BF16)
        kr = kr_ref[0]
        krope_s[...] = rope_part(kr * gkr_ref[...])
        ss = jnp.sum(jnp.where(first_half, kr * kr, 0.0), axis=-1, keepdims=True)
        krss_s[...] = jnp.broadcast_to(ss, (seq, LANES))

    gq = gq_ref[...]
    xqs = [_dot(cqn_s[...], wq_ref[h]) for h in heads]
    sqs = [lax.rsqrt(jnp.sum(x[:, :LANES] * x[:, :LANES] + jnp.where(first_half, x[:, LANES:] * x[:, LANES:], 0.0),
                             axis=-1, keepdims=True) * (1.0 / MLA_QK) + EPS) * (MLA_QK ** -0.5) for x in xqs]
    for h, x, sq in zip(heads, xqs, sqs):
        q_s[h] = jnp.concatenate(
            [x[:, :LANES] * (sq * gq[:, :LANES]), rope_part(x[:, LANES:] * (sq * gq[:, LANES:]))],
            axis=1).astype(BF16)
    xkvs = [_dot(ckvn_s[...], wkv_ref[h]) for h in heads]
    sks = [lax.rsqrt((jnp.sum(x[:, :LANES] * x[:, :LANES], axis=-1, keepdims=True) + krss_s[...])
                     * (1.0 / MLA_QK) + EPS) for x in xkvs]
    for h, x, sk in zip(heads, xkvs, sks):
        k_s[h] = jnp.concatenate([x[:, :LANES] * (sk * gkn_ref[...]), krope_s[...] * sk], axis=1).astype(BF16)
        v_s[h] = x[:, LANES:].astype(BF16)

    def q_block(r0, rows, nprev, diag):
        qs = [q_s[h, pl.ds(r0, rows), :] for h in heads]

        def scores(c0, size):
            return ([_dot_nt(q, k_s[h, pl.ds(c0, size), :]) for h, q in zip(heads, qs)],
                    [v_s[h, pl.ds(c0, size), :] for h in heads])

        def body(j, carry):
            return _softmax_steps(*scores(pl.multiple_of(j * kb, kb), kb), carry)

        init = (jnp.full((rows, 1), NEG_BIG, F32), jnp.zeros((rows, 1), F32), jnp.zeros((rows, MLA_V), F32))
        carry = lax.fori_loop(0, nprev, body, tuple(init for _ in heads))
        for off, size in diag:
            ss, vs = scores(r0 + off, size)
            r = lax.broadcasted_iota(jnp.int32, (rows, size), 0)
            c = lax.broadcasted_iota(jnp.int32, (rows, size), 1)
            carry = _softmax_steps([jnp.where(c + off <= r, s, NEG_BIG) for s in ss], vs, carry)
        for h, (_, l, acc) in zip(heads, carry):
            o_ref[0, pl.ds(r0, rows), h * MLA_V:(h + 1) * MLA_V] = (acc / l).astype(o_ref.dtype)

    nloop, last_r0, last_rows, diag, last_diag = _query_blocks(seq, qb, kb)

    def outer(i, _):
        q_block(pl.multiple_of(i * qb, qb), qb, i * (qb // kb), diag)
        return 0

    lax.fori_loop(0, nloop, outer, 0)
    q_block(last_r0, last_rows, last_r0 // kb, last_diag)


def _mla(p3d, g_cq, g_ckv, wq, wkv, gq, gkn, gkr, cs):
    bsz, seq, _ = p3d.shape
    hp = ATT_HEADS_PER_STEP
    full = lambda shape: pl.BlockSpec(shape, lambda b, g: (0,) * len(shape))
    return pl.pallas_call(
        functools.partial(_mla_kernel, seq=seq),
        grid=(bsz, MLA_HEADS // hp),
        in_specs=[
            pl.BlockSpec((1, seq, MLA_Q_RANK), lambda b, g: (b, 0, S5_WIDTH // MLA_Q_RANK)),
            pl.BlockSpec((1, seq, MLA_KV_RANK), lambda b, g: (b, 0, (S5_WIDTH + MLA_Q_RANK) // MLA_KV_RANK)),
            pl.BlockSpec((1, seq, LANES), lambda b, g: (b, 0, (S5_WIDTH + MLA_Q_RANK + MLA_KV_RANK) // LANES)),
            full((1, MLA_Q_RANK)), full((1, MLA_KV_RANK)),
            pl.BlockSpec((hp, MLA_Q_RANK, 2 * LANES), lambda b, g: (g, 0, 0)),
            pl.BlockSpec((hp, MLA_KV_RANK, 2 * LANES), lambda b, g: (g, 0, 0)),
            full((1, 2 * LANES)), full((1, LANES)), full((1, LANES)),
            full((seq, LANES)),
        ],
        out_specs=pl.BlockSpec((1, seq, hp * MLA_V), lambda b, g: (b, 0, g)),
        out_shape=jax.ShapeDtypeStruct((bsz, seq, MLA_HEADS * MLA_V), BF16),
        scratch_shapes=[pltpu.VMEM((seq, MLA_Q_RANK), BF16), pltpu.VMEM((seq, MLA_KV_RANK), BF16),
                        pltpu.VMEM((seq, LANES), F32), pltpu.VMEM((seq, LANES), F32),
                        pltpu.VMEM((hp, seq, 2 * LANES), BF16), pltpu.VMEM((hp, seq, 2 * LANES), BF16),
                        pltpu.VMEM((hp, seq, MLA_V), BF16)],
        compiler_params=_cparams("parallel", "arbitrary"),
        name="mla",
    )(p3d, p3d, p3d, g_cq, g_ckv, wq, wkv, gq, gkn, gkr, cs)


def _rot_cols(w):
    half = MLA_ROPE // 2
    return jnp.concatenate([-w[..., half:], w[..., :half]], axis=-1)


def _swap_halves(g):
    half = MLA_ROPE // 2
    return jnp.concatenate([g[..., half:], g[..., :half]], axis=-1)


def _mla_params(w_uq, w_ukv, g_q, g_k, seq):
    wq = w_uq.reshape(MLA_Q_RANK, MLA_HEADS, MLA_QK)
    wq = jnp.concatenate([wq, _rot_cols(wq[..., MLA_NOPE:])], axis=-1)
    wq = jnp.transpose(wq, (1, 0, 2)).astype(BF16)
    wkv = jnp.transpose(w_ukv.reshape(MLA_KV_RANK, MLA_HEADS, MLA_NOPE + MLA_V), (1, 0, 2)).astype(BF16)
    gq = jnp.concatenate([g_q, _swap_halves(g_q[MLA_NOPE:])]).reshape(1, 2 * LANES).astype(F32)
    gkn = g_k[:MLA_NOPE].reshape(1, LANES).astype(F32)
    gkr = jnp.concatenate([g_k[MLA_NOPE:], _swap_halves(g_k[MLA_NOPE:])]).reshape(1, LANES).astype(F32)
    half = MLA_ROPE // 2
    inv = ROPE_THETA ** (-jnp.arange(half, dtype=F32) / half)
    ang = jnp.arange(seq, dtype=F32)[:, None] * inv[None, :]
    cs = jnp.concatenate([jnp.cos(ang), jnp.cos(ang), jnp.sin(ang), jnp.sin(ang)], axis=1)
    return wq, wkv, gq, gkn, gkr, cs


def _split2(x):
    hi = x.astype(BF16)
    return hi, (x - hi.astype(F32)).astype(BF16)


def _split3(x):
    hi = x.astype(BF16)
    r = x - hi.astype(F32)
    mid = r.astype(BF16)
    return hi, mid, (r - mid.astype(F32)).astype(BF16)


def _sb_kernel(q_ref, k_ref, v_ref, o_ref, q_s, k_s, v_s, *, seq):
    qb, blk = ATT_Q_BLOCK, SB_BLOCK
    heads = range(q_s.shape[0])
    for h in heads:
        cols = slice(h * SB_DIM, (h + 1) * SB_DIM)
        q_s[h] = (q_ref[0, :, cols] * (SB_DIM ** -0.5)).astype(BF16)
        k_s[h] = k_ref[0, :, cols].astype(BF16)
        v_s[h] = v_ref[0, :, cols].astype(BF16)

    def later_keys(size):
        r = lax.broadcasted_iota(jnp.int32, (size, size), 0)
        c = lax.broadcasted_iota(jnp.int32, (size, size), 1)
        return (r > c).astype(BF16)

    def tiles(qs, c0, size, u_mat, strict, carries):
        tails, accs = zip(*carries)
        zs = [_dot_nt(q, k_s[h, pl.ds(c0, size), :]) for h, q in zip(heads, qs)]
        log_keeps = [-(jnp.maximum(z, 0.0) + jnp.log(1.0 + jnp.exp(-jnp.abs(z)))) for z in zs]
        if strict is not None:
            log_keeps = [jnp.where(strict, lk, 0.0) for lk in log_keeps]
        splits = [_split2(lk) for lk in log_keeps]
        inners = [_dot(hi, u_mat) + _dot(lo, u_mat) for hi, lo in splits]
        ws = [jnp.exp(z + lk + inner + tail) for z, lk, inner, tail in zip(zs, log_keeps, inners, tails)]
        if strict is not None:
            ws = [jnp.where(strict, w, 0.0) for w in ws]
        accs = [acc + _dot(w.astype(BF16), v_s[h, pl.ds(c0, size), :]) for h, w, acc in zip(heads, ws, accs)]
        tails = [tail + jnp.sum(lk, axis=-1, keepdims=True) for tail, lk in zip(tails, log_keeps)]
        return tuple(zip(tails, accs))

    u_blk = later_keys(blk)

    def q_block(r0, rows, nprev, diag):
        qs = [q_s[h, pl.ds(r0, rows), :] for h in heads]
        carry = tuple((jnp.zeros((rows, 1), F32), jnp.zeros((rows, SB_DIM), F32)) for _ in heads)
        for off, size, u_mat in reversed(diag):
            r = lax.broadcasted_iota(jnp.int32, (rows, size), 0)
            c = lax.broadcasted_iota(jnp.int32, (rows, size), 1)
            carry = tiles(qs, r0 + off, size, u_mat, c + off < r, carry)

        def live(cy):
            tail_max = functools.reduce(jnp.maximum, [tail for tail, _ in cy])
            return (jnp.max(tail_max) > SB_DEAD_TAIL).astype(jnp.int32)

        def body(state):
            jj, _, cy = state
            c0 = pl.multiple_of((nprev - 1 - jj) * blk, blk)
            cy = tiles(qs, c0, blk, u_blk, None, cy)
            return jj + 1, live(cy), cy

        _, _, carry = lax.while_loop(lambda s: (s[0] < nprev) & (s[1] > 0), body,
                                     (jnp.int32(0), live(carry), carry))
        for h, (_, acc) in zip(heads, carry):
            o_ref[0, pl.ds(r0, rows), h * SB_DIM:(h + 1) * SB_DIM] = acc.astype(o_ref.dtype)

    nloop, last_r0, last_rows, diag, last_diag = _query_blocks(seq, qb, blk)
    u_of = lambda size: u_blk if size == blk else later_keys(size)

    def outer(i, _):
        q_block(pl.multiple_of(i * qb, qb), qb, i * (qb // blk), [(o, s, u_of(s)) for o, s in diag])
        return 0

    lax.fori_loop(0, nloop, outer, 0)
    q_block(last_r0, last_rows, last_r0 // blk, [(o, s, u_of(s)) for o, s in last_diag])


def _stick_breaking(p3d):
    bsz, seq, _ = p3d.shape
    hp = ATT_HEADS_PER_STEP
    groups = SB_HEADS // hp
    spec = lambda off: pl.BlockSpec((1, seq, hp * SB_DIM), lambda b, g: (b, 0, off + g))
    return pl.pallas_call(
        functools.partial(_sb_kernel, seq=seq),
        grid=(bsz, groups),
        in_specs=[spec(0), spec(groups), spec(2 * groups)],
        out_specs=pl.BlockSpec((1, seq, hp * SB_DIM), lambda b, g: (b, 0, g)),
        out_shape=jax.ShapeDtypeStruct((bsz, seq, SB_W), BF16),
        scratch_shapes=[pltpu.VMEM((hp, seq, SB_DIM), BF16)] * 3,
        compiler_params=_cparams("parallel", "parallel"),
        name="stick_breaking",
    )(p3d, p3d, p3d)


def _gdn_kernel(q_ref, k_ref, v_ref, z_ref, ab_ref, cwq_ref, cwk_ref, cwv_ref, gate_ref, gout_ref, o_ref,
                pad_s, q_s, k_s, v_s, g_s, b_s, o_s, u_s, w_s, a_s, qd_s, kdt_s, gl_s, *, seq):
    c = GDN_CHUNK
    hp = q_s.shape[0]
    nchunks = pl.cdiv(seq, c)
    padded = nchunks * c
    pad_s[pl.ds(0, 8), :] = jnp.zeros((8, LANES), F32)

    def conv_silu(x_ref, w_ref, cols):
        pad_s[pl.ds(8, seq), :] = x_ref[0, :, cols]
        w = w_ref[:, cols]
        y = w[GDN_CONV - 1:GDN_CONV, :] * pad_s[pl.ds(8, seq), :]
        for j in range(GDN_CONV - 1):
            y = y + w[j:j + 1, :] * pad_s[pl.ds(8 - (GDN_CONV - 1) + j, seq), :]
        return y * _sigmoid(y)

    def l2norm(x):
        return x * lax.rsqrt(jnp.sum(x * x, axis=-1, keepdims=True) + EPS)

    ab = ab_ref[0]
    lane = lax.broadcasted_iota(jnp.int32, (seq, LANES), 1)
    gate = gate_ref[...]
    t = ab + gate[1:2, :]
    g_all = -jnp.exp(gate[0:1, :]) * (jnp.maximum(t, 0.0) + jnp.log(1.0 + jnp.exp(-jnp.abs(t))))
    beta_all = _sigmoid(ab)
    zeros_tail = jnp.zeros((padded - seq, LANES), F32)

    def prepare(hd):
        cols = slice(hd * LANES, (hd + 1) * LANES)
        head = pl.program_id(1) * hp + hd
        if padded > seq:
            for ref in (q_s, k_s, v_s, g_s, b_s):
                ref[hd, pl.ds(seq, padded - seq), :] = zeros_tail
        q_s[hd, pl.ds(0, seq), :] = l2norm(conv_silu(q_ref, cwq_ref, cols)) * (GDN_DK ** -0.5)
        k_s[hd, pl.ds(0, seq), :] = l2norm(conv_silu(k_ref, cwk_ref, cols))
        v_s[hd, pl.ds(0, seq), :] = conv_silu(v_ref, cwv_ref, cols)
        g_col = jnp.sum(jnp.where(lane == head, g_all, 0.0), axis=-1, keepdims=True)
        b_col = jnp.sum(jnp.where(lane == head + GDN_HEADS, beta_all, 0.0), axis=-1, keepdims=True)
        g_s[hd, pl.ds(0, seq), :] = jnp.broadcast_to(g_col, (seq, LANES))
        b_s[hd, pl.ds(0, seq), :] = jnp.broadcast_to(b_col, (seq, LANES))

    r = lax.broadcasted_iota(jnp.int32, (c, c), 0)
    col = lax.broadcasted_iota(jnp.int32, (c, c), 1)
    lower = r >= col
    strict = r > col
    incl = lower.astype(BF16)
    eye = (r == col).astype(F32)
    level_masks = [((r >> l) == (col >> l)) & ((r >> (l - 1)) != (col >> (l - 1))) & strict
                   for l in range(1, int(math.log2(c)) + 1)]

    def local(hd, first, count):
        ids = [first + t for t in range(count)]
        rows = [pl.ds(_aligned(i * c, c), c) for i in ids]
        ks = [k_s[hd, r, :] for r in rows]
        betas = [b_s[hd, r, :] for r in rows]
        gbs = [g_s[hd, r, :] for r in rows]
        sums = [_dot(incl, jnp.concatenate(_split3(gb) + _split3(jnp.where(strict, gb, 0.0)), axis=1))
                for gb in gbs]
        cums = [s[:, 0:c] + s[:, c:2 * c] + s[:, 2 * c:3 * c] for s in sums]
        decays = [jnp.exp(jnp.where(lower, s[:, 3 * c:4 * c] + s[:, 4 * c:5 * c] + s[:, 5 * c:6 * c], NEG_BIG))
                  for s in sums]
        kbs = [k * beta for k, beta in zip(ks, betas)]
        kbfs = [k.astype(BF16) for k in ks]
        ms = [jnp.where(strict, _dot_nt(kb.astype(BF16), kbf) * decay, 0.0)
              for kb, kbf, decay in zip(kbs, kbfs, decays)]
        invs = [eye.astype(BF16)] * count
        for mask in level_masks:
            ts = [_dot(jnp.where(mask, m, 0.0).astype(BF16), inv).astype(BF16) for m, inv in zip(ms, invs)]
            invs = [(inv.astype(F32) - _dot(inv, t)).astype(BF16) for inv, t in zip(invs, ts)]
        e_cums = [jnp.exp(cum) for cum in cums]
        for r, inv, kb, e_cum in zip(rows, invs, kbs, e_cums):
            w_s[hd, r, :] = _dot(inv, (kb * e_cum).astype(BF16)).astype(BF16)
        for r, inv, beta in zip(rows, invs, betas):
            u_s[hd, r, :] = _dot(inv, (v_s[hd, r, :] * beta).astype(BF16))
        qs = [q_s[hd, r, :] for r in rows]
        for r, q, kbf, decay in zip(rows, qs, kbfs, decays):
            a_s[hd, r, :] = jnp.where(lower, _dot_nt(q.astype(BF16), kbf) * decay, 0.0).astype(BF16)
        for i, r, q, k, cum, e_cum in zip(ids, rows, qs, ks, cums, e_cums):
            qd_s[hd, r, :] = (q * e_cum).astype(BF16)
            last = cum[c - 1:c, :]
            kdt_s[hd, r, :] = (k * jnp.exp(last - cum)).T.astype(BF16)
            gl_s[hd, pl.ds(_aligned(i * 8, 8), 8), :] = jnp.broadcast_to(jnp.exp(last), (8, LANES))

    def recur(i, states):
        rows = pl.ds(pl.multiple_of(i * c, c), c)
        heads = range(hp)
        sbs = [s.astype(BF16) for s in states]
        v_news = [(u_s[hd, rows, :] - _dot(w_s[hd, rows, :], sb)).astype(BF16) for hd, sb in zip(heads, sbs)]
        for hd, sb, v_new in zip(heads, sbs, v_news):
            o_s[hd, rows, :] = _dot(qd_s[hd, rows, :], sb) + _dot(a_s[hd, rows, :], v_new)
        return tuple(s * gl_s[hd, pl.ds(pl.multiple_of(i * 8, 8), 1), :] + _dot(kdt_s[hd, rows, :], v_new)
                     for hd, s, v_new in zip(heads, states, v_news))

    ngroups = max(nchunks // GDN_UNROLL, 1)
    for hd in range(hp):
        prepare(hd)
        for gi in range(ngroups):
            first = gi * GDN_UNROLL
            local(hd, first, GDN_UNROLL if gi < ngroups - 1 else nchunks - first)
    lax.fori_loop(0, nchunks, recur, tuple(jnp.zeros((GDN_DK, GDN_DV), F32) for _ in range(hp)))
    for hd in range(hp):
        cols = slice(hd * LANES, (hd + 1) * LANES)
        z = z_ref[0, :, cols]
        o = _rmsnorm(o_s[hd, pl.ds(0, seq), :], gout_ref[...]) * (z * _sigmoid(z))
        o_ref[0, :, cols] = o.astype(o_ref.dtype)


def _gated_deltanet(p3d, conv_w, gate, g_out):
    bsz, seq, _ = p3d.shape
    hp = GDN_HEADS_PER_STEP
    groups = GDN_HEADS // hp
    off = 3 * SB_HEADS // hp
    spec = lambda o: pl.BlockSpec((1, seq, hp * LANES), lambda b, g: (b, 0, o + g))
    cspec = lambda o: pl.BlockSpec((GDN_CONV, hp * LANES), lambda b, g: (0, o + g))
    padded = pl.cdiv(seq, GDN_CHUNK) * GDN_CHUNK
    return pl.pallas_call(
        functools.partial(_gdn_kernel, seq=seq),
        grid=(bsz, groups),
        in_specs=[
            spec(off), spec(off + groups), spec(off + 2 * groups), spec(off + 3 * groups),
            pl.BlockSpec((1, seq, LANES), lambda b, g: (b, 0, (3 * SB_HEADS + 4 * GDN_HEADS))),
            cspec(0), cspec(groups), cspec(2 * groups),
            pl.BlockSpec((2, LANES), lambda b, g: (0, 0)),
            pl.BlockSpec((1, LANES), lambda b, g: (0, 0)),
        ],
        out_specs=pl.BlockSpec((1, seq, hp * LANES), lambda b, g: (b, 0, g)),
        out_shape=jax.ShapeDtypeStruct((bsz, seq, GDN_HEADS * GDN_DV), BF16),
        scratch_shapes=([pltpu.VMEM((seq + 8, LANES), F32)] + [pltpu.VMEM((hp, padded, LANES), F32)] * 7
                        + [pltpu.VMEM((hp, padded, LANES), BF16)] * 4
                        + [pltpu.VMEM((hp, 8 * padded // GDN_CHUNK, LANES), F32)]),
        compiler_params=_cparams("parallel", "parallel"),
        name="gated_deltanet",
    )(p3d, p3d, p3d, p3d, p3d, conv_w, conv_w, conv_w, gate, g_out)


def _even_mixer(h, bsz, seq, gain, w_in, s5_params, w_glu, g_cq, g_ckv, w_uq, w_ukv, g_q, g_k, w_out):
    d = h.shape[1]
    rope0 = S5_WIDTH + MLA_Q_RANK + MLA_KV_RANK
    w_ext = jnp.concatenate([w_in, _rot_cols(w_in[:, rope0:])], axis=1).astype(BF16)
    p = _norm_matmul(h, gain, w_ext, tn=w_ext.shape[1], name="even_in_proj")
    p3d = p.reshape(bsz, seq, -1)
    y_a = _s5(p3d, _s5_tables(*s5_params))
    wq, wkv, gq, gkn, gkr, cs = _mla_params(w_uq, w_ukv, g_q, g_k, seq)
    y_b = _mla(p3d, g_cq.reshape(1, -1), g_ckv.reshape(1, -1), wq, wkv, gq, gkn, gkr, cs)
    return _out_proj(h, y_a.reshape(-1, S5_WIDTH), y_b.reshape(-1, MLA_HEADS * MLA_V),
                     w_out[:S5_WIDTH].astype(BF16), w_out[S5_WIDTH:].astype(BF16), w_glu.astype(BF16),
                     name="even_out_proj")


def _odd_mixer(h, bsz, seq, gain, w_in, conv_w, a_log, dt_bias, g_out, w_out):
    d = h.shape[1]
    o1 = 3 * SB_W + GDN_QKV
    ab = w_in[:, o1:o1 + 2 * GDN_HEADS]
    w_ext = jnp.concatenate(
        [w_in[:, :o1], w_in[:, o1 + 2 * GDN_HEADS:], ab, jnp.zeros((d, LANES - 2 * GDN_HEADS), w_in.dtype)],
        axis=1).astype(BF16)
    p = _norm_matmul(h, gain, w_ext, tn=w_ext.shape[1] // 3, name="odd_in_proj")
    p3d = p.reshape(bsz, seq, -1)
    y_c = _stick_breaking(p3d)
    pad = jnp.zeros((LANES - GDN_HEADS,), F32)
    gate = jnp.stack([jnp.concatenate([a_log.astype(F32), pad]), jnp.concatenate([dt_bias.astype(F32), pad])])
    y_d = _gated_deltanet(p3d, conv_w.astype(F32), gate, g_out.reshape(1, -1).astype(F32))
    return _out_proj(h, y_c.reshape(-1, SB_W), y_d.reshape(-1, GDN_HEADS * GDN_DV),
                     w_out[:SB_W].astype(BF16), w_out[SB_W:].astype(BF16), name="odd_out_proj")


def kernel(x, meta_tokens, norm_ffn1, w1_gate, w1_up, w1_down, norm_mix, norm_ffn2, w2_gate, w2_up, w2_down, ev_w_in, s5_log_dt, s5_a_re, s5_a_im, s5_b_re, s5_b_im, s5_c_re, s5_c_im, s5_d, s5_w_glu, mla_g_cq, mla_g_ckv, mla_w_uq, mla_w_ukv, mla_g_q, mla_g_k, ev_w_out, od_w_in, gdn_conv, gdn_a_log, gdn_dt_bias, gdn_g_out, od_w_out):
    bsz, _, d = x.shape
    depth = norm_ffn1.shape[0]
    meta = jnp.broadcast_to(meta_tokens[None].astype(x.dtype), (bsz, N_META, d))
    h3 = jnp.concatenate([meta, x], axis=1)
    seq = h3.shape[1]
    h = h3.reshape(bsz * seq, d)
    for l in range(depth):
        i = l // 2
        h = _ffn(h, norm_ffn1[l], w1_gate[l].astype(BF16), w1_up[l].astype(BF16), w1_down[l].astype(BF16))
        if l % 2 == 0:
            s5_params = (s5_log_dt[i], s5_a_re[i], s5_a_im[i], s5_b_re[i], s5_b_im[i], s5_c_re[i],
                         s5_c_im[i], s5_d[i])
            h = _even_mixer(h, bsz, seq, norm_mix[l], ev_w_in[i], s5_params, s5_w_glu[i], mla_g_cq[i],
                            mla_g_ckv[i], mla_w_uq[i], mla_w_ukv[i], mla_g_q[i], mla_g_k[i], ev_w_out[i])
        else:
            h = _odd_mixer(h, bsz, seq, norm_mix[l], od_w_in[i], gdn_conv[i], gdn_a_log[i], gdn_dt_bias[i],
                           gdn_g_out[i], od_w_out[i])
        h = _ffn(h, norm_ffn2[l], w2_gate[l].astype(BF16), w2_up[l].astype(BF16), w2_down[l].astype(BF16))
    return h.reshape(bsz, seq, d)[:, N_META:]
```
